```python
import math
import jax, jax.numpy as jnp
from jax import lax
import numpy as np

D_MODEL = 1024
BATCH = 8
SEQ = 4096
DEPTH = 2

CTX_LEN = 256
GRID_W = 64
D_MIX = D_MODEL
A_HEADS = 4
A_HEAD_DIM = 64
A_VDIM = 2 * A_HEAD_DIM
A_QK = A_HEADS * 2 * A_HEAD_DIM
A_WIDTH = A_HEADS * A_VDIM
B_WIDTH = 256
SHORT_CONV = 3
C_WIDTH = 256
CONF_CONV = 31
Q_BLOCK = 128
ROPE_BASE = 10000.0
EPS = 1e-6

Q_OFF = 0
K_OFF = Q_OFF + A_QK
V_OFF = K_OFF + A_QK
GA_OFF = V_OFF + A_WIDTH
BB_OFF = GA_OFF + A_WIDTH
BC_OFF = BB_OFF + B_WIDTH
BH_OFF = BC_OFF + B_WIDTH
GB_OFF = BH_OFF + B_WIDTH
CA_OFF = GB_OFF + B_WIDTH
CB_OFF = CA_OFF + C_WIDTH
GC_OFF = CB_OFF + C_WIDTH
D_IN = GC_OFF + C_WIDTH

kernel_name = "hybrid_diffattn_shortconv_conformer_dit"


def rmsnorm(x, g):
    xf = x.astype(jnp.float32)
    y = xf * lax.rsqrt(jnp.mean(xf * xf, axis=-1, keepdims=True) + EPS)
    return (y * g.astype(jnp.float32)).astype(x.dtype)


def layernorm(x, g, b):
    xf = x.astype(jnp.float32)
    mu = jnp.mean(xf, axis=-1, keepdims=True)
    xc = xf - mu
    var = jnp.mean(xc * xc, axis=-1, keepdims=True)
    return (xc * lax.rsqrt(var + EPS) * g.astype(jnp.float32) + b.astype(jnp.float32)).astype(x.dtype)


def modulation(cond, w_ada, b_ada):
    m = jax.nn.silu(cond) @ w_ada + b_ada
    return jnp.split(m, 3, axis=-1)


def axial_rope_tables(rows):
    r = jnp.repeat(jnp.arange(rows, dtype=jnp.float32), GRID_W)
    col = jnp.tile(jnp.arange(GRID_W, dtype=jnp.float32), rows)
    half = A_HEAD_DIM // 2
    inv_freq = ROPE_BASE ** (-jnp.arange(0, half, 2, dtype=jnp.float32) / half)
    ang = jnp.concatenate([r[:, None] * inv_freq, col[:, None] * inv_freq], axis=-1)
    return jnp.cos(ang), jnp.sin(ang)


def apply_axial_rope(t, cos, sin):
    tp = t.reshape(*t.shape[:-1], A_HEAD_DIM // 2, 2)
    t0, t1 = tp[..., 0], tp[..., 1]
    c = cos[None, :, None, None, :].astype(t.dtype)
    s = sin[None, :, None, None, :].astype(t.dtype)
    return jnp.stack([t0 * c - t1 * s, t0 * s + t1 * c], axis=-1).reshape(t.shape)


def dwconv(u, w):
    pad = w.shape[0] // 2
    return lax.conv_general_dilated(
        u, w[:, None, :].astype(u.dtype), window_strides=(1,), padding=[(pad, pad)],
        dimension_numbers=('NWC', 'WIO', 'NWC'), feature_group_count=u.shape[-1])


def diff_attend(q, k, v, lam):
    s = jnp.einsum('bqhmd,bkhmd->bhmqk', q, k, preferred_element_type=jnp.float32) * (A_HEAD_DIM ** -0.5)
    p = jax.nn.softmax(s, axis=-1)
    a = p[:, :, 0] - lam * p[:, :, 1]
    return jnp.einsum('bhqk,bkhe->bqhe', a.astype(v.dtype), v)


def latent_diff_attention(q, k_all, v_all, lam):
    b, s = q.shape[0], q.shape[1]
    nb = s // Q_BLOCK
    qb = q.reshape(b, nb, Q_BLOCK, *q.shape[2:]).swapaxes(0, 1)
    out = lax.map(lambda qi: diff_attend(qi, k_all, v_all, lam), qb)
    return out.swapaxes(0, 1).reshape(b, s, A_HEADS, A_VDIM)


def split_columns(p):
    return jnp.split(p, [K_OFF, V_OFF, GA_OFF, BB_OFF, BC_OFF, BH_OFF, GB_OFF, CA_OFF, CB_OFF, GC_OFF], axis=-1)


def heads_qk(t):
    return t.reshape(t.shape[0], t.shape[1], A_HEADS, 2, A_HEAD_DIM)


def heads_v(t):
    return t.reshape(t.shape[0], t.shape[1], A_HEADS, A_VDIM)


def attn_group_out(o, gate, g_sub, lam_init):
    o = rmsnorm(o, g_sub) * (1.0 - lam_init)
    return o.reshape(o.shape[0], o.shape[1], A_WIDTH) * jax.nn.silu(gate)


def conv_groups_out(bb, bc, bh, gb, ca, cb, gc, w_short, w_conf, b_conf, g_ln, b_ln):
    yb = bb * dwconv(bc * bh, w_short) * jax.nn.silu(gb)
    u = ca * jax.nn.sigmoid(cb)
    u = dwconv(u, w_conf) + b_conf
    yc = jax.nn.silu(layernorm(u, g_ln, b_ln)) * jax.nn.silu(gc)
    return yb, yc


def setup_inputs(seed: int = 0) -> dict:
    key = jax.random.key(seed)
    ks = jax.random.split(key, 24)
    f32 = jnp.float32
    n = lambda k, shape: jax.random.normal(k, shape, f32)
    return {
        'x': n(ks[0], (BATCH, SEQ, D_MODEL)),
        'c': n(ks[1], (BATCH, D_MODEL)),
        'ctx': n(ks[2], (BATCH, CTX_LEN, D_MODEL)),
        'c_ctx': 0.5 * n(ks[3], (D_MODEL,)),
        'w_ada': n(ks[4], (DEPTH, D_MODEL, 3 * D_MODEL)) * (0.5 * D_MODEL ** -0.5),
        'b_ada': 0.02 * n(ks[5], (DEPTH, 3 * D_MODEL)),
        'g_norm': 1.0 + 0.02 * n(ks[6], (DEPTH, D_MODEL)),
        'w_in': n(ks[7], (DEPTH, D_MODEL, D_IN)) * D_MODEL ** -0.5,
        'lam_q1': 0.1 * n(ks[8], (DEPTH, A_HEAD_DIM)),
        'lam_k1': 0.1 * n(ks[9], (DEPTH, A_HEAD_DIM)),
        'lam_q2': 0.1 * n(ks[10], (DEPTH, A_HEAD_DIM)),
        'lam_k2': 0.1 * n(ks[11], (DEPTH, A_HEAD_DIM)),
        'g_subln': 1.0 + 0.02 * n(ks[12], (DEPTH, A_VDIM)),
        'w_short': n(ks[13], (DEPTH, SHORT_CONV, B_WIDTH)) * SHORT_CONV ** -0.5,
        'w_conf': n(ks[14], (DEPTH, CONF_CONV, C_WIDTH)) * CONF_CONV ** -0.5,
        'b_conf': 0.02 * n(ks[15], (DEPTH, C_WIDTH)),
        'g_conf_ln': 1.0 + 0.02 * n(ks[16], (DEPTH, C_WIDTH)),
        'b_conf_ln': 0.02 * n(ks[17], (DEPTH, C_WIDTH)),
        'w_out': n(ks[18], (DEPTH, D_MIX, D_MODEL)) * D_MIX ** -0.5,
        'g_final': 1.0 + 0.02 * n(ks[19], (D_MODEL,)),
    }


def reference(x, c, ctx, c_ctx, w_ada, b_ada, g_norm, w_in, lam_q1, lam_k1, lam_q2, lam_k2,
              g_subln, w_short, w_conf, b_conf, g_conf_ln, b_conf_ln, w_out, g_final):
    n_tok = x.shape[1]
    ROWS = n_tok // GRID_W
    cos, sin = axial_rope_tables(ROWS)
    f32 = jnp.float32
    for l in range(DEPTH):
        last = l == DEPTH - 1
        lam_init = 0.8 - 0.6 * math.exp(-0.3 * l)
        lam = (jnp.exp(jnp.sum(lam_q1[l].astype(f32) * lam_k1[l].astype(f32)))
               - jnp.exp(jnp.sum(lam_q2[l].astype(f32) * lam_k2[l].astype(f32))) + lam_init)

        sh, sc, gt = modulation(c, w_ada[l], b_ada[l])
        sh_c, sc_c, gt_c = modulation(c_ctx, w_ada[l], b_ada[l])
        hx = rmsnorm(x, g_norm[l]) * (1.0 + sc[:, None, :]) + sh[:, None, :]
        hc = rmsnorm(ctx, g_norm[l]) * (1.0 + sc_c) + sh_c

        qx, kx, vx, gax, bbx, bcx, bhx, gbx, cax, cbx, gcx = split_columns(hx @ w_in[l])
        if not last:
            qc, kc, vc, gac, bbc, bcc, bhc, gbc, cac, cbc, gcc = split_columns(hc @ w_in[l])
        else:
            kc = hc @ w_in[l][:, K_OFF:V_OFF]
            vc = hc @ w_in[l][:, V_OFF:GA_OFF]
        kc_h, vc_h = heads_qk(kc), heads_v(vc)

        qx_h = apply_axial_rope(heads_qk(qx), cos, sin)
        kx_h = apply_axial_rope(heads_qk(kx), cos, sin)
        k_all = jnp.concatenate([kc_h, kx_h], axis=1)
        v_all = jnp.concatenate([vc_h, heads_v(vx)], axis=1)
        ya = attn_group_out(latent_diff_attention(qx_h, k_all, v_all, lam), gax, g_subln[l], lam_init)
        yb, yc = conv_groups_out(bbx, bcx, bhx, gbx, cax, cbx, gcx, w_short[l], w_conf[l],
                                 b_conf[l], g_conf_ln[l], b_conf_ln[l])
        y = jnp.concatenate([ya, yb, yc], axis=-1) @ w_out[l]

        if not last:
            ya_c = attn_group_out(diff_attend(heads_qk(qc), kc_h, vc_h, lam), gac, g_subln[l], lam_init)
            yb_c, yc_c = conv_groups_out(bbc, bcc, bhc, gbc, cac, cbc, gcc, w_short[l], w_conf[l],
                                         b_conf[l], g_conf_ln[l], b_conf_ln[l])
            y_c = jnp.concatenate([ya_c, yb_c, yc_c], axis=-1) @ w_out[l]
            ctx = ctx + gt_c * y_c
        x = x + gt[:, None, :] * y
    return rmsnorm(x, g_final)
```

```python
import functools
import math

import jax
import jax.numpy as jnp
from jax import lax
from jax.experimental import pallas as pl
from jax.experimental.pallas import tpu as pltpu

F32 = jnp.float32
BF16 = jnp.bfloat16

D_MODEL = 1024
GRID_W = 64
N_HEADS = 4
HEAD_DIM = 64
HEAD_QK = 2 * HEAD_DIM
V_DIM = 128
A_WIDTH = N_HEADS * V_DIM
B_WIDTH = 256
C_WIDTH = 256
SHORT_CONV = 3
CONF_CONV = 31
ROPE_BASE = 10000.0
EPS = 1e-6

Q_OFF = 0
K_OFF = 512
V_OFF = 1024
GA_OFF = 1536
B_OFF = 2048
C_OFF = 3072
D_IN = 3840

LANES = 128
BF16_SUBLANES = 16
V_ROWS = V_DIM + BF16_SUBLANES
CONV_HALO = 16
LOG2E = 1.4426950408889634
VMEM_LIMIT = 56 * 1024 * 1024


def _silu(t):
    return t * jax.nn.sigmoid(t)


def _cparams(n_axes):
    return pltpu.CompilerParams(dimension_semantics=("parallel",) * n_axes,
                                vmem_limit_bytes=VMEM_LIMIT)


def _mod_kernel(cond_ref, w_ref, b_ref, o_ref):
    a = _silu(cond_ref[...])
    o_ref[0] = jnp.dot(a.astype(BF16), w_ref[0].astype(BF16),
                       preferred_element_type=F32) + b_ref[0]


def _modulation(cond, w_ada, b_ada):
    depth, d, n = w_ada.shape
    r = cond.shape[0]
    bn = 1024
    return pl.pallas_call(
        _mod_kernel,
        grid=(depth, n // bn),
        in_specs=[pl.BlockSpec((r, d), lambda l, j: (0, 0)),
                  pl.BlockSpec((1, d, bn), lambda l, j: (l, 0, j)),
                  pl.BlockSpec((1, 1, bn), lambda l, j: (l, 0, j))],
        out_specs=pl.BlockSpec((1, r, bn), lambda l, j: (l, 0, j)),
        out_shape=jax.ShapeDtypeStruct((depth, r, n), F32),
        compiler_params=_cparams(2),
        name="modulation",
    )(cond, w_ada, b_ada.reshape(depth, 1, n))


def _rope(t, cos, sa, sb):
    return t * cos + pltpu.roll(t, LANES - 1, 1) * sa + pltpu.roll(t, 1, 1) * sb


def _store_vt(vt_ref, v):
    tm = v.shape[0]
    row = lax.broadcasted_iota(jnp.int32, (BF16_SUBLANES, tm), 0)
    ones_rows = jnp.where(row == 0, 1.0, 0.0).astype(BF16)
    for h in range(N_HEADS):
        vt_ref[0, h, 0:V_DIM, :] = v[:, h * V_DIM:(h + 1) * V_DIM].T.astype(BF16)
        vt_ref[0, h, V_DIM:V_ROWS, :] = ones_rows


def _inproj_kernel(*refs, rope, kv_only, q_scale):
    x_ref, sc_ref, sh_ref, g_ref, w_ref = refs[:5]
    refs = refs[5:]
    if rope:
        cos_ref, sa_ref, sb_ref = refs[:3]
        refs = refs[3:]
        cos, sa, sb = cos_ref[...], sa_ref[...], sb_ref[...]

    x = x_ref[0]
    ms = jnp.mean(x * x, axis=-1, keepdims=True)
    h = x * lax.rsqrt(ms + EPS) * g_ref[...]
    hb = (h * sc_ref[0] + sh_ref[0]).astype(BF16)

    def proj(lo, hi):
        return jnp.dot(hb, w_ref[:, lo:hi], preferred_element_type=F32)

    def store_qk(dst_ref, t, scale):
        for j in range(t.shape[1] // LANES):
            tj = t[:, j * LANES:(j + 1) * LANES]
            if rope:
                tj = _rope(tj, cos, sa, sb)
            if scale != 1.0:
                tj = tj * scale
            dst_ref[0, :, j * LANES:(j + 1) * LANES] = tj.astype(BF16)

    if kv_only:
        k_ref, vt_ref = refs
        store_qk(k_ref, proj(0, 512), 1.0)
        _store_vt(vt_ref, proj(512, 1024))
        return

    q_ref, k_ref, vt_ref, ga_ref, cv_ref = refs
    store_qk(q_ref, proj(Q_OFF, K_OFF), q_scale)
    store_qk(k_ref, proj(K_OFF, V_OFF), 1.0)
    _store_vt(vt_ref, proj(V_OFF, GA_OFF))
    ga_ref[0] = proj(GA_OFF, B_OFF).astype(BF16)

    pb = proj(B_OFF, C_OFF)
    bb, bc = pb[:, 0:256], pb[:, 256:512]
    bh, gb = pb[:, 512:768], pb[:, 768:1024]
    cv_ref[0, :, 0:256] = (bb * _silu(gb)).astype(BF16)
    cv_ref[0, :, 256:512] = (bc * bh).astype(BF16)
    pc = proj(C_OFF, D_IN)
    ca, cb, gc = pc[:, 0:256], pc[:, 256:512], pc[:, 512:768]
    cv_ref[0, :, 512:768] = (ca * jax.nn.sigmoid(cb)).astype(BF16)
    cv_ref[0, :, 768:1024] = _silu(gc).astype(BF16)


def _inproj(x, sc1p, sh, g, w, rope_tabs, *, kv_only, tm):
    b, s, d = x.shape
    rope = rope_tabs is not None
    n = w.shape[1]
    grid = (b, s // tm)
    in_specs = [pl.BlockSpec((1, tm, d), lambda i, j: (i, j, 0)),
                pl.BlockSpec((1, 1, d), lambda i, j: (i, 0, 0)),
                pl.BlockSpec((1, 1, d), lambda i, j: (i, 0, 0)),
                pl.BlockSpec((1, d), lambda i, j: (0, 0)),
                pl.BlockSpec((d, n), lambda i, j: (0, 0))]
    args = [x, sc1p, sh, g, w]
    if rope:
        in_specs += [pl.BlockSpec((tm, LANES), lambda i, j: (j, 0))] * 3
        args += list(rope_tabs)
    tok = lambda width: (jax.ShapeDtypeStruct((b, s, width), BF16),
                         pl.BlockSpec((1, tm, width), lambda i, j: (i, j, 0)))
    vt = (jax.ShapeDtypeStruct((b, N_HEADS, V_ROWS, s), BF16),
          pl.BlockSpec((1, N_HEADS, V_ROWS, tm), lambda i, j: (i, 0, 0, j)))
    outs = [tok(512), vt] if kv_only else [tok(512), tok(512), vt, tok(512), tok(1024)]
    q_scale = LOG2E * HEAD_DIM ** -0.5
    return pl.pallas_call(
        functools.partial(_inproj_kernel, rope=rope, kv_only=kv_only, q_scale=q_scale),
        grid=grid,
        in_specs=in_specs,
        out_specs=[o[1] for o in outs],
        out_shape=[o[0] for o in outs],
        compiler_params=_cparams(2),
        name="inproj",
    )(*args)


def _attn_kernel(*refs, seg_lens, chunk, lam_init):
    n_seg = len(seg_lens)
    q_ref, ga_ref, lamp_ref, gsub_ref = refs[:4]
    k_refs = refs[4:4 + n_seg]
    vt_refs = refs[4 + n_seg:4 + 2 * n_seg]
    o_ref = refs[4 + 2 * n_seg]
    s_scr = refs[5 + 2 * n_seg]

    q = q_ref[0]
    lane = lax.broadcasted_iota(jnp.int32, q.shape, 1)
    zero = jnp.zeros_like(q)
    q_maps = (jnp.where(lane < HEAD_DIM, q, zero), jnp.where(lane >= HEAD_DIM, q, zero))

    chunks = []
    off = 0
    for si, n in enumerate(seg_lens):
        for c0 in range(0, n, chunk):
            chunks.append((si, c0, min(chunk, n - c0), off + c0))
        off += n

    acc = []
    for m in range(2):
        mx = None
        for si, c0, cn, r0 in chunks:
            s = lax.dot_general(k_refs[si][0, c0:c0 + cn, :], q_maps[m],
                                (((1,), (1,)), ((), ())), preferred_element_type=F32)
            s_scr[m, r0:r0 + cn, :] = s
            cm = jnp.max(s, axis=0, keepdims=True)
            mx = cm if mx is None else jnp.maximum(mx, cm)
        a = None
        for si, c0, cn, r0 in chunks:
            e = jnp.exp2(s_scr[m, r0:r0 + cn, :] - mx).astype(BF16)
            pv = jnp.dot(vt_refs[si][0, 0, :, c0:c0 + cn], e, preferred_element_type=F32)
            a = pv if a is None else a + pv
        acc.append(a)

    lp = lamp_ref[...]
    lam = (jnp.exp(jnp.sum(lp[0:1] * lp[1:2], axis=-1, keepdims=True))
           - jnp.exp(jnp.sum(lp[2:3] * lp[3:4], axis=-1, keepdims=True)) + lam_init)
    r0 = 1.0 / acc[0][V_DIM:V_DIM + 1]
    r1 = lam / acc[1][V_DIM:V_DIM + 1]
    o = (acc[0][0:V_DIM] * r0 - acc[1][0:V_DIM] * r1).T
    ms = jnp.mean(o * o, axis=-1, keepdims=True)
    y = o * lax.rsqrt(ms + EPS) * gsub_ref[...] * (1.0 - lam_init)
    o_ref[0] = (y * _silu(ga_ref[0].astype(F32))).astype(BF16)


def _attention(q, ga, lam_params, g_sub, ks, vts, *, lam_init, qb):
    b, s, _ = q.shape
    seg_lens = tuple(k.shape[1] for k in ks)
    ktot = sum(seg_lens)
    grid = (b, N_HEADS, s // qb)
    in_specs = [pl.BlockSpec((1, qb, HEAD_QK), lambda i, h, j: (i, j, h)),
                pl.BlockSpec((1, qb, V_DIM), lambda i, h, j: (i, j, h)),
                pl.BlockSpec((4, HEAD_DIM), lambda i, h, j: (0, 0)),
                pl.BlockSpec((1, V_DIM), lambda i, h, j: (0, 0))]
    in_specs += [pl.BlockSpec((1, n, HEAD_QK), lambda i, h, j: (i, 0, h)) for n in seg_lens]
    in_specs += [pl.BlockSpec((1, 1, V_ROWS, n), lambda i, h, j: (i, h, 0, 0)) for n in seg_lens]
    return pl.pallas_call(
        functools.partial(_attn_kernel, seg_lens=seg_lens, chunk=512, lam_init=lam_init),
        grid=grid,
        in_specs=in_specs,
        out_specs=pl.BlockSpec((1, qb, V_DIM), lambda i, h, j: (i, j, h)),
        out_shape=jax.ShapeDtypeStruct((b, s, A_WIDTH), BF16),
        scratch_shapes=[pltpu.VMEM((2, ktot, qb), F32)],
        compiler_params=_cparams(3),
        name="attention",
    )(q, ga, lam_params, g_sub, *ks, *vts)


def _taps(win, t, taps):
    rows = win.shape[0]
    acc = None
    for b in range(8):
        group = [(off, w) for off, w in taps if off % 8 == b]
        if not group:
            continue
        wb = win if b == 0 else pltpu.roll(win, rows - b, 0)
        for off, w in group:
            term = wb[off - b:off - b + t] * w
            acc = term if acc is None else acc + term
    return acc


def _conv_kernel(cv_ref, wsh_ref, wcf_ref, bcf_ref, gln_ref, bln_ref, o_ref, ub_scr, uc_scr, *, s, t):
    c = pl.program_id(1)

    @pl.when(c == 0)
    def _():
        zeros = jnp.zeros((CONV_HALO, B_WIDTH), F32)
        for scr, lo in ((ub_scr, 256), (uc_scr, 512)):
            scr[0:CONV_HALO, :] = zeros
            scr[CONV_HALO + s:2 * CONV_HALO + s, :] = zeros
            scr[CONV_HALO:CONV_HALO + s, :] = cv_ref[0, :, lo:lo + 256].astype(F32)

    base = pl.multiple_of(c * t, 8)
    half = SHORT_CONV // 2
    win_b = ub_scr[pl.ds(base + 8, t + 16), :]
    conv_b = _taps(win_b, t, [(8 - half + k, wsh_ref[k:k + 1, :]) for k in range(SHORT_CONV)])
    half = CONF_CONV // 2
    win_c = uc_scr[pl.ds(base, t + 32), :]
    conv_c = _taps(win_c, t, [(CONV_HALO - half + k, wcf_ref[k:k + 1, :]) for k in range(CONF_CONV)])

    tok = pl.ds(base, t)
    yb = cv_ref[0, tok, 0:256].astype(F32) * conv_b
    u = conv_c + bcf_ref[...]
    mu = jnp.mean(u, axis=-1, keepdims=True)
    uc = u - mu
    var = jnp.mean(uc * uc, axis=-1, keepdims=True)
    ln = uc * lax.rsqrt(var + EPS) * gln_ref[...] + bln_ref[...]
    yc = _silu(ln) * cv_ref[0, tok, 768:1024].astype(F32)
    o_ref[0, :, 0:256] = yb.astype(BF16)
    o_ref[0, :, 256:512] = yc.astype(BF16)


def _conv_groups(cv, w_short, w_conf, b_conf, g_ln, b_ln, *, t):
    b, s, _ = cv.shape
    vec = lambda rows: pl.BlockSpec((rows, 256), lambda i, j: (0, 0))
    return pl.pallas_call(
        functools.partial(_conv_kernel, s=s, t=t),
        grid=(b, s // t),
        in_specs=[pl.BlockSpec((1, s, 1024), lambda i, j: (i, 0, 0)),
                  vec(SHORT_CONV), vec(CONF_CONV), vec(1), vec(1), vec(1)],
        out_specs=pl.BlockSpec((1, t, 512), lambda i, j: (i, j, 0)),
        out_shape=jax.ShapeDtypeStruct((b, s, 512), BF16),
        scratch_shapes=[pltpu.VMEM((s + 2 * CONV_HALO, B_WIDTH), F32),
                        pltpu.VMEM((s + 2 * CONV_HALO, C_WIDTH), F32)],
        compiler_params=pltpu.CompilerParams(dimension_semantics=("parallel", "arbitrary"),
                                             vmem_limit_bytes=VMEM_LIMIT),
        name="conv_groups",
    )(cv, w_short, w_conf, b_conf.reshape(1, -1), g_ln.reshape(1, -1), b_ln.reshape(1, -1))


def _outproj_kernel(ya_ref, ybc_ref, w_ref, x_ref, gt_ref, *rest, final):
    y = (jnp.dot(ya_ref[0], w_ref[0:A_WIDTH, :], preferred_element_type=F32)
         + jnp.dot(ybc_ref[0], w_ref[A_WIDTH:, :], preferred_element_type=F32))
    xn = x_ref[0] + gt_ref[0] * y
    if final:
        gf_ref, o_ref = rest
        ms = jnp.mean(xn * xn, axis=-1, keepdims=True)
        o_ref[0] = xn * lax.rsqrt(ms + EPS) * gf_ref[...]
    else:
        (o_ref,) = rest
        o_ref[0] = xn


def _outproj(ya, ybc, w, x, gt, g_final, *, tm):
    b, s, d = x.shape
    tok = lambda width: pl.BlockSpec((1, tm, width), lambda i, j: (i, j, 0))
    in_specs = [tok(512), tok(512), pl.BlockSpec((d, d), lambda i, j: (0, 0)), tok(d),
                pl.BlockSpec((1, 1, d), lambda i, j: (i, 0, 0))]
    args = [ya, ybc, w, x, gt]
    final = g_final is not None
    if final:
        in_specs.append(pl.BlockSpec((1, d), lambda i, j: (0, 0)))
        args.append(g_final.reshape(1, d))
    return pl.pallas_call(
        functools.partial(_outproj_kernel, final=final),
        grid=(b, s // tm),
        in_specs=in_specs,
        out_specs=tok(d),
        out_shape=jax.ShapeDtypeStruct((b, s, d), F32),
        compiler_params=_cparams(2),
        name="outproj",
    )(*args)


def _rope_tables(s):
    rows = s // GRID_W
    r = jnp.repeat(jnp.arange(rows, dtype=F32), GRID_W)
    col = jnp.tile(jnp.arange(GRID_W, dtype=F32), rows)
    half = HEAD_DIM // 2
    inv_freq = ROPE_BASE ** (-jnp.arange(0, half, 2, dtype=F32) / half)
    ang = jnp.concatenate([r[:, None] * inv_freq, col[:, None] * inv_freq], axis=-1)
    cos = jnp.repeat(jnp.cos(ang), 2, axis=-1)
    sin = jnp.repeat(jnp.sin(ang), 2, axis=-1)
    even = (jnp.arange(HEAD_DIM) % 2 == 0)[None, :]
    sa = jnp.where(even, -sin, 0.0)
    sb = jnp.where(even, 0.0, sin)
    return tuple(jnp.tile(t, (1, LANES // HEAD_DIM)) for t in (cos, sa, sb))


def kernel(x, c, ctx, c_ctx, w_ada, b_ada, g_norm, w_in, lam_q1, lam_k1, lam_q2, lam_k2,
           g_subln, w_short, w_conf, b_conf, g_conf_ln, b_conf_ln, w_out, g_final):
    b, s, d = x.shape
    n_ctx = ctx.shape[1]
    depth = w_in.shape[0]
    tm_x = min(512, s)
    tm_c = min(256, n_ctx)
    qb_x = min(256, s)
    qb_c = min(256, n_ctx)

    r = -(-(b + 1) // 8) * 8
    cond = jnp.zeros((r, d), F32).at[:b].set(c).at[b].set(c_ctx)
    mods = _modulation(cond, w_ada, b_ada)
    rope_tabs = _rope_tables(s)
    w_in_b = w_in.astype(BF16)
    w_out_b = w_out.astype(BF16)

    for l in range(depth):
        last = l == depth - 1
        lam_init = 0.8 - 0.6 * math.exp(-0.3 * l)
        lam_params = jnp.stack([lam_q1[l], lam_k1[l], lam_q2[l], lam_k2[l]]).astype(F32)
        g_sub = g_subln[l].reshape(1, V_DIM)
        g_l = g_norm[l].reshape(1, d)
        m = mods[l]
        sh, sc, gt = m[:b, None, 0:d], m[:b, None, d:2 * d], m[:b, None, 2 * d:3 * d]
        bc = lambda v: jnp.broadcast_to(v[None, None, :], (b, 1, d))
        sh_c, sc_c, gt_c = bc(m[b, 0:d]), bc(m[b, d:2 * d]), bc(m[b, 2 * d:3 * d])
        conv_w = (w_short[l], w_conf[l], b_conf[l], g_conf_ln[l], b_conf_ln[l])

        qx, kx, vtx, gax, cvx = _inproj(x, 1.0 + sc, sh, g_l, w_in_b[l], rope_tabs,
                                        kv_only=False, tm=tm_x)
        if not last:
            qc, kc, vtc, gac, cvc = _inproj(ctx, 1.0 + sc_c, sh_c, g_l, w_in_b[l], None,
                                            kv_only=False, tm=tm_c)
        else:
            kc, vtc = _inproj(ctx, 1.0 + sc_c, sh_c, g_l, w_in_b[l][:, K_OFF:GA_OFF], None,
                              kv_only=True, tm=tm_c)

        ya = _attention(qx, gax, lam_params, g_sub, (kc, kx), (vtc, vtx), lam_init=lam_init, qb=qb_x)
        ybc = _conv_groups(cvx, *conv_w, t=min(256, s))
        x_new = _outproj(ya, ybc, w_out_b[l], x, gt, g_final if last else None, tm=tm_x)

        if not last:
            ya_c = _attention(qc, gac, lam_params, g_sub, (kc,), (vtc,), lam_init=lam_init, qb=qb_c)
            ybc_c = _conv_groups(cvc, *conv_w, t=min(256, n_ctx))
            ctx = _outproj(ya_c, ybc_c, w_out_b[l], ctx, gt_c, None, tm=tm_c)
        x = x_new
    return x
```

```python
import functools
import math

import jax
import jax.numpy as jnp
from jax import lax
from jax.experimental import pallas as pl
from jax.experimental.pallas import tpu as pltpu

F32 = jnp.float32
BF16 = jnp.bfloat16

D_MODEL = 1024
GRID_W = 64
N_HEADS = 4
HEAD_DIM = 64
HEAD_QK = 2 * HEAD_DIM
V_DIM = 128
A_WIDTH = N_HEADS * V_DIM
B_WIDTH = 256
C_WIDTH = 256
SHORT_CONV = 3
CONF_CONV = 31
ROPE_BASE = 10000.0
EPS = 1e-6

Q_OFF = 0
K_OFF = 512
V_OFF = 1024
GA_OFF = 1536
B_OFF = 2048
C_OFF = 3072
D_IN = 3840

LANES = 128
BF16_SUBLANES = 16
V_ROWS = V_DIM + BF16_SUBLANES
CONV_HALO = 16
LOG2E = 1.4426950408889634
VMEM_LIMIT = 56 * 1024 * 1024


def _silu(t):
    return t * jax.nn.sigmoid(t)


def _cparams(n_axes):
    return pltpu.CompilerParams(dimension_semantics=("parallel",) * n_axes,
                                vmem_limit_bytes=VMEM_LIMIT)


def _mod_kernel(cond_ref, w_ref, b_ref, o_ref):
    a = _silu(cond_ref[...])
    o_ref[0] = jnp.dot(a.astype(BF16), w_ref[0].astype(BF16),
                       preferred_element_type=F32) + b_ref[0]


def _modulation(cond, w_ada, b_ada):
    depth, d, n = w_ada.shape
    r = cond.shape[0]
    bn = 1024
    return pl.pallas_call(
        _mod_kernel,
        grid=(depth, n // bn),
        in_specs=[pl.BlockSpec((r, d), lambda l, j: (0, 0)),
                  pl.BlockSpec((1, d, bn), lambda l, j: (l, 0, j)),
                  pl.BlockSpec((1, 1, bn), lambda l, j: (l, 0, j))],
        out_specs=pl.BlockSpec((1, r, bn), lambda l, j: (l, 0, j)),
        out_shape=jax.ShapeDtypeStruct((depth, r, n), F32),
        compiler_params=_cparams(2),
        name="modulation",
    )(cond, w_ada, b_ada.reshape(depth, 1, n))


def _rope(t, cos, sa, sb):
    return t * cos + pltpu.roll(t, LANES - 1, 1) * sa + pltpu.roll(t, 1, 1) * sb


def _store_vt(vt_ref, v):
    tm = v.shape[0]
    row = lax.broadcasted_iota(jnp.int32, (BF16_SUBLANES, tm), 0)
    ones_rows = jnp.where(row == 0, 1.0, 0.0).astype(BF16)
    for h in range(N_HEADS):
        vt_ref[0, h, 0:V_DIM, :] = v[:, h * V_DIM:(h + 1) * V_DIM].T.astype(BF16)
        vt_ref[0, h, V_DIM:V_ROWS, :] = ones_rows


def _inproj_kernel(*refs, rope, kv_only, q_scale):
    x_ref, sc_ref, sh_ref, g_ref, w_ref = refs[:5]
    refs = refs[5:]
    if rope:
        cos_ref, sa_ref, sb_ref = refs[:3]
        refs = refs[3:]
        cos, sa, sb = cos_ref[...], sa_ref[...], sb_ref[...]

    x = x_ref[0]
    ms = jnp.mean(x * x, axis=-1, keepdims=True)
    h = x * lax.rsqrt(ms + EPS) * g_ref[...]
    hb = (h * sc_ref[0] + sh_ref[0]).astype(BF16)

    def proj(lo, hi):
        return jnp.dot(hb, w_ref[:, lo:hi], preferred_element_type=F32)

    def store_qk(dst_ref, t, scale):
        for j in range(t.shape[1] // LANES):
            tj = t[:, j * LANES:(j + 1) * LANES]
            if rope:
                tj = _rope(tj, cos, sa, sb)
            if scale != 1.0:
                tj = tj * scale
            dst_ref[0, :, j * LANES:(j + 1) * LANES] = tj.astype(BF16)

    if kv_only:
        k_ref, vt_ref = refs
        store_qk(k_ref, proj(0, 512), 1.0)
        _store_vt(vt_ref, proj(512, 1024))
        return

    q_ref, k_ref, vt_ref, ga_ref, cv_ref = refs
    store_qk(q_ref, proj(Q_OFF, K_OFF), q_scale)
    store_qk(k_ref, proj(K_OFF, V_OFF), 1.0)
    _store_vt(vt_ref, proj(V_OFF, GA_OFF))
    ga_ref[0] = proj(GA_OFF, B_OFF).astype(BF16)

    pb = proj(B_OFF, C_OFF)
    bb, bc = pb[:, 0:256], pb[:, 256:512]
    bh, gb = pb[:, 512:768], pb[:, 768:1024]
    cv_ref[0, :, 0:256] = (bb * _silu(gb)).astype(BF16)
    cv_ref[0, :, 256:512] = (bc * bh).astype(BF16)
    pc = proj(C_OFF, D_IN)
    ca, cb, gc = pc[:, 0:256], pc[:, 256:512], pc[:, 512:768]
    cv_ref[0, :, 512:768] = (ca * jax.nn.sigmoid(cb)).astype(BF16)
    cv_ref[0, :, 768:1024] = _silu(gc).astype(BF16)


def _inproj(x, sc1p, sh, g, w, rope_tabs, *, kv_only, tm):
    b, s, d = x.shape
    rope = rope_tabs is not None
    n = w.shape[1]
    grid = (b, s // tm)
    in_specs = [pl.BlockSpec((1, tm, d), lambda i, j: (i, j, 0)),
                pl.BlockSpec((1, 1, d), lambda i, j: (i, 0, 0)),
                pl.BlockSpec((1, 1, d), lambda i, j: (i, 0, 0)),
                pl.BlockSpec((1, d), lambda i, j: (0, 0)),
                pl.BlockSpec((d, n), lambda i, j: (0, 0))]
    args = [x, sc1p, sh, g, w]
    if rope:
        in_specs += [pl.BlockSpec((tm, LANES), lambda i, j: (j, 0))] * 3
        args += list(rope_tabs)
    tok = lambda width: (jax.ShapeDtypeStruct((b, s, width), BF16),
                         pl.BlockSpec((1, tm, width), lambda i, j: (i, j, 0)))
    vt = (jax.ShapeDtypeStruct((b, N_HEADS, V_ROWS, s), BF16),
          pl.BlockSpec((1, N_HEADS, V_ROWS, tm), lambda i, j: (i, 0, 0, j)))
    outs = [tok(512), vt] if kv_only else [tok(512), tok(512), vt, tok(512), tok(1024)]
    q_scale = LOG2E * HEAD_DIM ** -0.5
    return pl.pallas_call(
        functools.partial(_inproj_kernel, rope=rope, kv_only=kv_only, q_scale=q_scale),
        grid=grid,
        in_specs=in_specs,
        out_specs=[o[1] for o in outs],
        out_shape=[o[0] for o in outs],
        compiler_params=_cparams(2),
        name="inproj",
    )(*args)


def _attn_kernel(*refs, seg_lens, chunk, lam_init):
    n_seg = len(seg_lens)
    q_ref, ga_ref, lamp_ref, gsub_ref = refs[:4]
    k_refs = refs[4:4 + n_seg]
    vt_refs = refs[4 + n_seg:4 + 2 * n_seg]
    o_ref, s_even, mx_even, s_odd, mx_odd = refs[4 + 2 * n_seg:]
    t = pl.program_id(0)

    @pl.when(t == 0)
    def _():
        s_odd[...] = jnp.zeros(s_odd.shape, F32)
        mx_odd[...] = jnp.zeros(mx_odd.shape, F32)

    chunks = []
    off = 0
    for si, n in enumerate(seg_lens):
        for c0 in range(0, n, chunk):
            chunks.append((si, c0, min(chunk, n - c0), off + c0))
        off += n

    def step(s_wr, mx_wr, s_rd, mx_rd):
        q = q_ref[0]
        lane = lax.broadcasted_iota(jnp.int32, q.shape, 1)
        zero = jnp.zeros_like(q)
        q_maps = (jnp.where(lane < HEAD_DIM, q, zero), jnp.where(lane >= HEAD_DIM, q, zero))
        acc = []
        for m in range(2):
            mx_new = None
            mx_old = mx_rd[m][0:1]
            a = None
            for si, c0, cn, r0 in chunks:
                s = lax.dot_general(k_refs[si][0, c0:c0 + cn, :], q_maps[m],
                                    (((1,), (1,)), ((), ())), preferred_element_type=F32)
                s_wr[m, r0:r0 + cn, :] = s
                cm = jnp.max(s, axis=0, keepdims=True)
                mx_new = cm if mx_new is None else jnp.maximum(mx_new, cm)

                e = jnp.exp2(s_rd[m, r0:r0 + cn, :] - mx_old).astype(BF16)
                pv = jnp.dot(vt_refs[si][0, 0, :, c0:c0 + cn], e, preferred_element_type=F32)
                a = pv if a is None else a + pv
            mx_wr[m] = jnp.broadcast_to(mx_new, mx_wr.shape[1:])
            acc.append(a)

        lp = lamp_ref[...]
        lam = (jnp.exp(jnp.sum(lp[0:1] * lp[1:2], axis=-1, keepdims=True))
               - jnp.exp(jnp.sum(lp[2:3] * lp[3:4], axis=-1, keepdims=True)) + lam_init)
        r0 = 1.0 / acc[0][V_DIM:V_DIM + 1]
        r1 = lam / acc[1][V_DIM:V_DIM + 1]
        o = (acc[0][0:V_DIM] * r0 - acc[1][0:V_DIM] * r1).T
        ms = jnp.mean(o * o, axis=-1, keepdims=True)
        y = o * lax.rsqrt(ms + EPS) * gsub_ref[...] * (1.0 - lam_init)
        o_ref[0] = (y * _silu(ga_ref[0].astype(F32))).astype(BF16)

    @pl.when(t % 2 == 0)
    def _():
        step(s_even, mx_even, s_odd, mx_odd)

    @pl.when(t % 2 == 1)
    def _():
        step(s_odd, mx_odd, s_even, mx_even)


def _attention(q, ga, lam_params, g_sub, ks, vts, *, lam_init, qb):
    b, s, _ = q.shape
    seg_lens = tuple(k.shape[1] for k in ks)
    ktot = sum(seg_lens)
    nq = s // qb
    n_units = b * N_HEADS * nq

    def unit(u):
        return u // (N_HEADS * nq), (u // nq) % N_HEADS, u % nq

    def a_idx(t):
        return unit(jnp.minimum(t, n_units - 1))

    def b_idx(t):
        return unit(jnp.maximum(t - 1, 0))

    def tok_a(t):
        i, h, j = a_idx(t)
        return i, j, h

    def tok_b(t):
        i, h, j = b_idx(t)
        return i, j, h

    def keys_a(t):
        i, h, _ = a_idx(t)
        return i, 0, h

    def vals_b(t):
        i, h, _ = b_idx(t)
        return i, h, 0, 0

    in_specs = [pl.BlockSpec((1, qb, HEAD_QK), tok_a),
                pl.BlockSpec((1, qb, V_DIM), tok_b),
                pl.BlockSpec((4, HEAD_DIM), lambda t: (0, 0)),
                pl.BlockSpec((1, V_DIM), lambda t: (0, 0))]
    in_specs += [pl.BlockSpec((1, n, HEAD_QK), keys_a) for n in seg_lens]
    in_specs += [pl.BlockSpec((1, 1, V_ROWS, n), vals_b) for n in seg_lens]
    return pl.pallas_call(
        functools.partial(_attn_kernel, seg_lens=seg_lens, chunk=512, lam_init=lam_init),
        grid=(n_units + 1,),
        in_specs=in_specs,
        out_specs=pl.BlockSpec((1, qb, V_DIM), tok_b),
        out_shape=jax.ShapeDtypeStruct((b, s, A_WIDTH), BF16),
        scratch_shapes=[pltpu.VMEM((2, ktot, qb), F32), pltpu.VMEM((2, 8, qb), F32)] * 2,
        compiler_params=pltpu.CompilerParams(dimension_semantics=("arbitrary",),
                                             vmem_limit_bytes=VMEM_LIMIT),
        name="attention",
    )(q, ga, lam_params, g_sub, *ks, *vts)


def _taps(win, t, taps):
    rows = win.shape[0]
    acc = None
    for b in range(8):
        group = [(off, w) for off, w in taps if off % 8 == b]
        if not group:
            continue
        wb = win if b == 0 else pltpu.roll(win, rows - b, 0)
        for off, w in group:
            term = wb[off - b:off - b + t] * w
            acc = term if acc is None else acc + term
    return acc


def _conv_kernel(cv_ref, wsh_ref, wcf_ref, bcf_ref, gln_ref, bln_ref, o_ref, ub_scr, uc_scr, *, s, t):
    c = pl.program_id(1)

    @pl.when(c == 0)
    def _():
        zeros = jnp.zeros((CONV_HALO, B_WIDTH), F32)
        for scr, lo in ((ub_scr, 256), (uc_scr, 512)):
            scr[0:CONV_HALO, :] = zeros
            scr[CONV_HALO + s:2 * CONV_HALO + s, :] = zeros
            scr[CONV_HALO:CONV_HALO + s, :] = cv_ref[0, :, lo:lo + 256].astype(F32)

    base = pl.multiple_of(c * t, 8)
    half = SHORT_CONV // 2
    win_b = ub_scr[pl.ds(base + 8, t + 16), :]
    conv_b = _taps(win_b, t, [(8 - half + k, wsh_ref[k:k + 1, :]) for k in range(SHORT_CONV)])
    half = CONF_CONV // 2
    win_c = uc_scr[pl.ds(base, t + 32), :]
    conv_c = _taps(win_c, t, [(CONV_HALO - half + k, wcf_ref[k:k + 1, :]) for k in range(CONF_CONV)])

    tok = pl.ds(base, t)
    yb = cv_ref[0, tok, 0:256].astype(F32) * conv_b
    u = conv_c + bcf_ref[...]
    mu = jnp.mean(u, axis=-1, keepdims=True)
    uc = u - mu
    var = jnp.mean(uc * uc, axis=-1, keepdims=True)
    ln = uc * lax.rsqrt(var + EPS) * gln_ref[...] + bln_ref[...]
    yc = _silu(ln) * cv_ref[0, tok, 768:1024].astype(F32)
    o_ref[0, :, 0:256] = yb.astype(BF16)
    o_ref[0, :, 256:512] = yc.astype(BF16)


def _conv_groups(cv, w_short, w_conf, b_conf, g_ln, b_ln, *, t):
    b, s, _ = cv.shape
    vec = lambda rows: pl.BlockSpec((rows, 256), lambda i, j: (0, 0))
    return pl.pallas_call(
        functools.partial(_conv_kernel, s=s, t=t),
        grid=(b, s // t),
        in_specs=[pl.BlockSpec((1, s, 1024), lambda i, j: (i, 0, 0)),
                  vec(SHORT_CONV), vec(CONF_CONV), vec(1), vec(1), vec(1)],
        out_specs=pl.BlockSpec((1, t, 512), lambda i, j: (i, j, 0)),
        out_shape=jax.ShapeDtypeStruct((b, s, 512), BF16),
        scratch_shapes=[pltpu.VMEM((s + 2 * CONV_HALO, B_WIDTH), F32),
                        pltpu.VMEM((s + 2 * CONV_HALO, C_WIDTH), F32)],
        compiler_params=pltpu.CompilerParams(dimension_semantics=("parallel", "arbitrary"),
                                             vmem_limit_bytes=VMEM_LIMIT),
        name="conv_groups",
    )(cv, w_short, w_conf, b_conf.reshape(1, -1), g_ln.reshape(1, -1), b_ln.reshape(1, -1))


def _outproj_kernel(ya_ref, ybc_ref, w_ref, x_ref, gt_ref, *rest, final):
    y = (jnp.dot(ya_ref[0], w_ref[0:A_WIDTH, :], preferred_element_type=F32)
         + jnp.dot(ybc_ref[0], w_ref[A_WIDTH:, :], preferred_element_type=F32))
    xn = x_ref[0] + gt_ref[0] * y
    if final:
        gf_ref, o_ref = rest
        ms = jnp.mean(xn * xn, axis=-1, keepdims=True)
        o_ref[0] = xn * lax.rsqrt(ms + EPS) * gf_ref[...]
    else:
        (o_ref,) = rest
        o_ref[0] = xn


def _outproj(ya, ybc, w, x, gt, g_final, *, tm):
    b, s, d = x.shape
    tok = lambda width: pl.BlockSpec((1, tm, width), lambda i, j: (i, j, 0))
    in_specs = [tok(512), tok(512), pl.BlockSpec((d, d), lambda i, j: (0, 0)), tok(d),
                pl.BlockSpec((1, 1, d), lambda i, j: (i, 0, 0))]
    args = [ya, ybc, w, x, gt]
    final = g_final is not None
    if final:
        in_specs.append(pl.BlockSpec((1, d), lambda i, j: (0, 0)))
        args.append(g_final.reshape(1, d))
    return pl.pallas_call(
        functools.partial(_outproj_kernel, final=final),
        grid=(b, s // tm),
        in_specs=in_specs,
        out_specs=tok(d),
        out_shape=jax.ShapeDtypeStruct((b, s, d), F32),
        compiler_params=_cparams(2),
        name="outproj",
    )(*args)


def _rope_tables(s):
    rows = s // GRID_W
    r = jnp.repeat(jnp.arange(rows, dtype=F32), GRID_W)
    col = jnp.tile(jnp.arange(GRID_W, dtype=F32), rows)
    half = HEAD_DIM // 2
    inv_freq = ROPE_BASE ** (-jnp.arange(0, half, 2, dtype=F32) / half)
    ang = jnp.concatenate([r[:, None] * inv_freq, col[:, None] * inv_freq], axis=-1)
    cos = jnp.repeat(jnp.cos(ang), 2, axis=-1)
    sin = jnp.repeat(jnp.sin(ang), 2, axis=-1)
    even = (jnp.arange(HEAD_DIM) % 2 == 0)[None, :]
    sa = jnp.where(even, -sin, 0.0)
    sb = jnp.where(even, 0.0, sin)
    return tuple(jnp.tile(t, (1, LANES // HEAD_DIM)) for t in (cos, sa, sb))


def kernel(x, c, ctx, c_ctx, w_ada, b_ada, g_norm, w_in, lam_q1, lam_k1, lam_q2, lam_k2,
           g_subln, w_short, w_conf, b_conf, g_conf_ln, b_conf_ln, w_out, g_final):
    b, s, d = x.shape
    n_ctx = ctx.shape[1]
    depth = w_in.shape[0]
    tm_x = min(512, s)
    tm_c = min(256, n_ctx)
    qb_x = min(256, s)
    qb_c = min(256, n_ctx)

    r = -(-(b + 1) // 8) * 8
    cond = jnp.zeros((r, d), F32).at[:b].set(c).at[b].set(c_ctx)
    mods = _modulation(cond, w_ada, b_ada)
    rope_tabs = _rope_tables(s)
    w_in_b = w_in.astype(BF16)
    w_out_b = w_out.astype(BF16)

    for l in range(depth):
        last = l == depth - 1
        lam_init = 0.8 - 0.6 * math.exp(-0.3 * l)
        lam_params = jnp.stack([lam_q1[l], lam_k1[l], lam_q2[l], lam_k2[l]]).astype(F32)
        g_sub = g_subln[l].reshape(1, V_DIM)
        g_l = g_norm[l].reshape(1, d)
        m = mods[l]
        sh, sc, gt = m[:b, None, 0:d], m[:b, None, d:2 * d], m[:b, None, 2 * d:3 * d]
        bc = lambda v: jnp.broadcast_to(v[None, None, :], (b, 1, d))
        sh_c, sc_c, gt_c = bc(m[b, 0:d]), bc(m[b, d:2 * d]), bc(m[b, 2 * d:3 * d])
        conv_w = (w_short[l], w_conf[l], b_conf[l], g_conf_ln[l], b_conf_ln[l])

        qx, kx, vtx, gax, cvx = _inproj(x, 1.0 + sc, sh, g_l, w_in_b[l], rope_tabs,
                                        kv_only=False, tm=tm_x)
        if not last:
            qc, kc, vtc, gac, cvc = _inproj(ctx, 1.0 + sc_c, sh_c, g_l, w_in_b[l], None,
                                            kv_only=False, tm=tm_c)
        else:
            kc, vtc = _inproj(ctx, 1.0 + sc_c, sh_c, g_l, w_in_b[l][:, K_OFF:GA_OFF], None,
                              kv_only=True, tm=tm_c)

        ya = _attention(qx, gax, lam_params, g_sub, (kc, kx), (vtc, vtx), lam_init=lam_init, qb=qb_x)
        ybc = _conv_groups(cvx, *conv_w, t=min(256, s))
        x_new = _outproj(ya, ybc, w_out_b[l], x, gt, g_final if last else None, tm=tm_x)

        if not last:
            ya_c = _attention(qc, gac, lam_params, g_sub, (kc,), (vtc,), lam_init=lam_init, qb=qb_c)
            ybc_c = _conv_groups(cvc, *conv_w, t=min(256, n_ctx))
            ctx = _outproj(ya_c, ybc_c, w_out_b[l], ctx, gt_c, None, tm=tm_c)
        x = x_new
    return x
```

```python
import functools
import math

import jax
import jax.numpy as jnp
from jax import lax
from jax.experimental import pallas as pl
from jax.experimental.pallas import tpu as pltpu

F32 = jnp.float32
BF16 = jnp.bfloat16

D_MODEL = 1024
GRID_W = 64
N_HEADS = 4
HEAD_DIM = 64
HEAD_QK = 2 * HEAD_DIM
V_DIM = 128
A_WIDTH = N_HEADS * V_DIM
B_WIDTH = 256
C_WIDTH = 256
SHORT_CONV = 3
CONF_CONV = 31
ROPE_BASE = 10000.0
EPS = 1e-6

Q_OFF = 0
K_OFF = 512
V_OFF = 1024
GA_OFF = 1536
B_OFF = 2048
C_OFF = 3072
D_IN = 3840

LANES = 128
CONV_HALO = 16
LOG2E = 1.4426950408889634
VMEM_LIMIT = 56 * 1024 * 1024


def _silu(t):
    return t * jax.nn.sigmoid(t)


def _cparams(n_axes):
    return pltpu.CompilerParams(dimension_semantics=("parallel",) * n_axes,
                                vmem_limit_bytes=VMEM_LIMIT)


def _mod_kernel(cond_ref, w_ref, b_ref, o_ref):
    a = _silu(cond_ref[...])
    o_ref[0] = jnp.dot(a.astype(BF16), w_ref[0].astype(BF16),
                       preferred_element_type=F32) + b_ref[0]


def _modulation(cond, w_ada, b_ada):
    depth, d, n = w_ada.shape
    r = cond.shape[0]
    bn = 1024
    return pl.pallas_call(
        _mod_kernel,
        grid=(depth, n // bn),
        in_specs=[pl.BlockSpec((r, d), lambda l, j: (0, 0)),
                  pl.BlockSpec((1, d, bn), lambda l, j: (l, 0, j)),
                  pl.BlockSpec((1, 1, bn), lambda l, j: (l, 0, j))],
        out_specs=pl.BlockSpec((1, r, bn), lambda l, j: (l, 0, j)),
        out_shape=jax.ShapeDtypeStruct((depth, r, n), F32),
        compiler_params=_cparams(2),
        name="modulation",
    )(cond, w_ada, b_ada.reshape(depth, 1, n))


def _rope(t, cos, sa, sb):
    return t * cos + pltpu.roll(t, LANES - 1, 1) * sa + pltpu.roll(t, 1, 1) * sb


def _store_heads_t(dst_ref, v):
    for h in range(N_HEADS):
        dst_ref[0, h] = v[:, h * V_DIM:(h + 1) * V_DIM].T.astype(BF16)


def _inproj_kernel(*refs, rope, kv_only, q_scale):
    x_ref, sc_ref, sh_ref, g_ref, w_ref = refs[:5]
    refs = refs[5:]
    if rope:
        cos_ref, sa_ref, sb_ref = refs[:3]
        refs = refs[3:]
        cos, sa, sb = cos_ref[...], sa_ref[...], sb_ref[...]

    x = x_ref[0]
    ms = jnp.mean(x * x, axis=-1, keepdims=True)
    h = x * lax.rsqrt(ms + EPS) * g_ref[...]
    hb = (h * sc_ref[0] + sh_ref[0]).astype(BF16)

    def proj(lo, hi):
        return jnp.dot(hb, w_ref[:, lo:hi], preferred_element_type=F32)

    def store_qk(dst_ref, t, scale):
        for j in range(t.shape[1] // LANES):
            tj = t[:, j * LANES:(j + 1) * LANES]
            if rope:
                tj = _rope(tj, cos, sa, sb)
            if scale != 1.0:
                tj = tj * scale
            dst_ref[0, :, j * LANES:(j + 1) * LANES] = tj.astype(BF16)

    if kv_only:
        k_ref, vt_ref = refs
        store_qk(k_ref, proj(0, 512), 1.0)
        _store_heads_t(vt_ref, proj(512, 1024))
        return

    q_ref, k_ref, vt_ref, gat_ref, cv_ref = refs
    store_qk(q_ref, proj(Q_OFF, K_OFF), q_scale)
    store_qk(k_ref, proj(K_OFF, V_OFF), 1.0)
    _store_heads_t(vt_ref, proj(V_OFF, GA_OFF))
    _store_heads_t(gat_ref, proj(GA_OFF, B_OFF))

    pb = proj(B_OFF, C_OFF)
    bb, bc = pb[:, 0:256], pb[:, 256:512]
    bh, gb = pb[:, 512:768], pb[:, 768:1024]
    cv_ref[0, :, 0:256] = (bb * _silu(gb)).astype(BF16)
    cv_ref[0, :, 256:512] = (bc * bh).astype(BF16)
    pc = proj(C_OFF, D_IN)
    ca, cb, gc = pc[:, 0:256], pc[:, 256:512], pc[:, 512:768]
    cv_ref[0, :, 512:768] = (ca * jax.nn.sigmoid(cb)).astype(BF16)
    cv_ref[0, :, 768:1024] = _silu(gc).astype(BF16)


def _inproj(x, sc1p, sh, g, w, rope_tabs, *, kv_only, tm):
    b, s, d = x.shape
    rope = rope_tabs is not None
    n = w.shape[1]
    grid = (b, s // tm)
    in_specs = [pl.BlockSpec((1, tm, d), lambda i, j: (i, j, 0)),
                pl.BlockSpec((1, 1, d), lambda i, j: (i, 0, 0)),
                pl.BlockSpec((1, 1, d), lambda i, j: (i, 0, 0)),
                pl.BlockSpec((1, d), lambda i, j: (0, 0)),
                pl.BlockSpec((d, n), lambda i, j: (0, 0))]
    args = [x, sc1p, sh, g, w]
    if rope:
        in_specs += [pl.BlockSpec((tm, LANES), lambda i, j: (j, 0))] * 3
        args += list(rope_tabs)
    tok = lambda width: (jax.ShapeDtypeStruct((b, s, width), BF16),
                         pl.BlockSpec((1, tm, width), lambda i, j: (i, j, 0)))
    heads_t = (jax.ShapeDtypeStruct((b, N_HEADS, V_DIM, s), BF16),
               pl.BlockSpec((1, N_HEADS, V_DIM, tm), lambda i, j: (i, 0, 0, j)))
    outs = [tok(512), heads_t] if kv_only else [tok(512), tok(512), heads_t, heads_t, tok(1024)]
    q_scale = LOG2E * HEAD_DIM ** -0.5
    return pl.pallas_call(
        functools.partial(_inproj_kernel, rope=rope, kv_only=kv_only, q_scale=q_scale),
        grid=grid,
        in_specs=in_specs,
        out_specs=[o[1] for o in outs],
        out_shape=[o[0] for o in outs],
        compiler_params=_cparams(2),
        name="inproj",
    )(*args)


def _attn_kernel(*refs, seg_lens, chunk, lam_init):
    n_seg = len(seg_lens)
    q_ref, lamp_ref = refs[:2]
    k_refs = refs[2:2 + n_seg]
    vt_refs = refs[2 + n_seg:2 + 2 * n_seg]
    o_ref = refs[2 + 2 * n_seg]
    even, odd = refs[3 + 2 * n_seg:7 + 2 * n_seg], refs[7 + 2 * n_seg:]
    t = pl.program_id(0)

    @pl.when(t == 0)
    def _():
        s_odd, mx_odd, acc_odd, den_odd = odd
        s_odd[...] = jnp.zeros(s_odd.shape, F32)
        mx_odd[...] = jnp.zeros(mx_odd.shape, F32)
        acc_odd[...] = jnp.zeros(acc_odd.shape, F32)
        den_odd[...] = jnp.ones(den_odd.shape, F32)

    chunks = []
    off = 0
    for si, n in enumerate(seg_lens):
        for c0 in range(0, n, chunk):
            chunks.append((si, c0, min(chunk, n - c0), off + c0))
        off += n

    def step(wr, rd):
        s_wr, mx_wr, acc_wr, den_wr = wr
        s_rd, mx_rd, acc_rd, den_rd = rd

        lp = lamp_ref[...]
        lam = (jnp.exp(jnp.sum(lp[0:1] * lp[1:2], axis=-1, keepdims=True))
               - jnp.exp(jnp.sum(lp[2:3] * lp[3:4], axis=-1, keepdims=True)) + lam_init)
        r0 = 1.0 / den_rd[0][0:1]
        r1 = lam / den_rd[1][0:1]
        o_ref[0, 0] = (acc_rd[0] * r0 - acc_rd[1] * r1).astype(BF16)

        q = q_ref[0]
        lane = lax.broadcasted_iota(jnp.int32, q.shape, 1)
        zero = jnp.zeros_like(q)
        q_maps = (jnp.where(lane < HEAD_DIM, q, zero), jnp.where(lane >= HEAD_DIM, q, zero))
        for m in range(2):
            mx_new = None
            mx_old = mx_rd[m][0:1]
            a = None
            den = None
            for si, c0, cn, r0 in chunks:
                s = lax.dot_general(k_refs[si][0, c0:c0 + cn, :], q_maps[m],
                                    (((1,), (1,)), ((), ())), preferred_element_type=F32)
                s_wr[m, r0:r0 + cn, :] = s
                cm = jnp.max(s, axis=0, keepdims=True)
                mx_new = cm if mx_new is None else jnp.maximum(mx_new, cm)

                e = jnp.exp2(s_rd[m, r0:r0 + cn, :] - mx_old)
                ds = jnp.sum(e, axis=0, keepdims=True)
                den = ds if den is None else den + ds
                pv = jnp.dot(vt_refs[si][0, 0, :, c0:c0 + cn], e.astype(BF16), preferred_element_type=F32)
                a = pv if a is None else a + pv
            mx_wr[m] = jnp.broadcast_to(mx_new, mx_wr.shape[1:])
            den_wr[m] = jnp.broadcast_to(den, den_wr.shape[1:])
            acc_wr[m] = a

    @pl.when(t % 2 == 0)
    def _():
        step(even, odd)

    @pl.when(t % 2 == 1)
    def _():
        step(odd, even)


def _attention(q, lam_params, ks, vts, *, lam_init, qb):
    b, s, _ = q.shape
    seg_lens = tuple(k.shape[1] for k in ks)
    ktot = sum(seg_lens)
    nq = s // qb
    n_units = b * N_HEADS * nq

    def unit(t, lag):
        u = jnp.clip(t - lag, 0, n_units - 1)
        return u // (N_HEADS * nq), (u // nq) % N_HEADS, u % nq

    def q_a(t):
        i, h, j = unit(t, 0)
        return i, j, h

    def out_c(t):
        i, h, j = unit(t, 2)
        return i, h, 0, j

    def keys_a(t):
        i, h, _ = unit(t, 0)
        return i, 0, h

    def vals_b(t):
        i, h, _ = unit(t, 1)
        return i, h, 0, 0

    in_specs = [pl.BlockSpec((1, qb, HEAD_QK), q_a),
                pl.BlockSpec((4, HEAD_DIM), lambda t: (0, 0))]
    in_specs += [pl.BlockSpec((1, n, HEAD_QK), keys_a) for n in seg_lens]
    in_specs += [pl.BlockSpec((1, 1, V_DIM, n), vals_b) for n in seg_lens]
    parity_scratch = [pltpu.VMEM((2, ktot, qb), F32), pltpu.VMEM((2, 8, qb), F32),
                      pltpu.VMEM((2, V_DIM, qb), F32), pltpu.VMEM((2, 8, qb), F32)]
    return pl.pallas_call(
        functools.partial(_attn_kernel, seg_lens=seg_lens, chunk=512, lam_init=lam_init),
        grid=(n_units + 2,),
        in_specs=in_specs,
        out_specs=pl.BlockSpec((1, 1, V_DIM, qb), out_c),
        out_shape=jax.ShapeDtypeStruct((b, N_HEADS, V_DIM, s), BF16),
        scratch_shapes=parity_scratch * 2,
        compiler_params=pltpu.CompilerParams(dimension_semantics=("arbitrary",),
                                             vmem_limit_bytes=VMEM_LIMIT),
        name="attention",
    )(q, lam_params, *ks, *vts)


def _taps(win, t, taps):
    rows = win.shape[0]
    acc = None
    for b in range(8):
        group = [(off, w) for off, w in taps if off % 8 == b]
        if not group:
            continue
        wb = win if b == 0 else pltpu.roll(win, rows - b, 0)
        for off, w in group:
            term = wb[off - b:off - b + t] * w
            acc = term if acc is None else acc + term
    return acc


def _conv_kernel(cv_ref, wsh_ref, wcf_ref, bcf_ref, gln_ref, bln_ref, o_ref, ub_scr, uc_scr, *, s, t):
    c = pl.program_id(1)

    @pl.when(c == 0)
    def _():
        zeros = jnp.zeros((CONV_HALO, B_WIDTH), F32)
        for scr, lo in ((ub_scr, 256), (uc_scr, 512)):
            scr[0:CONV_HALO, :] = zeros
            scr[CONV_HALO + s:2 * CONV_HALO + s, :] = zeros
            scr[CONV_HALO:CONV_HALO + s, :] = cv_ref[0, :, lo:lo + 256].astype(F32)

    base = pl.multiple_of(c * t, 8)
    half = SHORT_CONV // 2
    win_b = ub_scr[pl.ds(base + 8, t + 16), :]
    conv_b = _taps(win_b, t, [(8 - half + k, wsh_ref[k:k + 1, :]) for k in range(SHORT_CONV)])
    half = CONF_CONV // 2
    win_c = uc_scr[pl.ds(base, t + 32), :]
    conv_c = _taps(win_c, t, [(CONV_HALO - half + k, wcf_ref[k:k + 1, :]) for k in range(CONF_CONV)])

    tok = pl.ds(base, t)
    yb = cv_ref[0, tok, 0:256].astype(F32) * conv_b
    u = conv_c + bcf_ref[...]
    mu = jnp.mean(u, axis=-1, keepdims=True)
    uc = u - mu
    var = jnp.mean(uc * uc, axis=-1, keepdims=True)
    ln = uc * lax.rsqrt(var + EPS) * gln_ref[...] + bln_ref[...]
    yc = _silu(ln) * cv_ref[0, tok, 768:1024].astype(F32)
    o_ref[0, :, 0:256] = yb.astype(BF16)
    o_ref[0, :, 256:512] = yc.astype(BF16)


def _conv_groups(cv, w_short, w_conf, b_conf, g_ln, b_ln, *, t):
    b, s, _ = cv.shape
    vec = lambda rows: pl.BlockSpec((rows, 256), lambda i, j: (0, 0))
    return pl.pallas_call(
        functools.partial(_conv_kernel, s=s, t=t),
        grid=(b, s // t),
        in_specs=[pl.BlockSpec((1, s, 1024), lambda i, j: (i, 0, 0)),
                  vec(SHORT_CONV), vec(CONF_CONV), vec(1), vec(1), vec(1)],
        out_specs=pl.BlockSpec((1, t, 512), lambda i, j: (i, j, 0)),
        out_shape=jax.ShapeDtypeStruct((b, s, 512), BF16),
        scratch_shapes=[pltpu.VMEM((s + 2 * CONV_HALO, B_WIDTH), F32),
                        pltpu.VMEM((s + 2 * CONV_HALO, C_WIDTH), F32)],
        compiler_params=pltpu.CompilerParams(dimension_semantics=("parallel", "arbitrary"),
                                             vmem_limit_bytes=VMEM_LIMIT),
        name="conv_groups",
    )(cv, w_short, w_conf, b_conf.reshape(1, -1), g_ln.reshape(1, -1), b_ln.reshape(1, -1))


def _outproj_kernel(ot_ref, gat_ref, gsub_ref, ybc_ref, w_ref, x_ref, gt_ref, *rest, final, lam_init):
    y = jnp.dot(ybc_ref[0], w_ref[A_WIDTH:, :], preferred_element_type=F32)
    g = gsub_ref[...] * (1.0 - lam_init)
    for h in range(N_HEADS):
        o = ot_ref[0, h].astype(F32)
        ms = jnp.mean(o * o, axis=0, keepdims=True)
        ya = o * lax.rsqrt(ms + EPS) * g * _silu(gat_ref[0, h].astype(F32))
        y = y + lax.dot_general(ya.astype(BF16), w_ref[h * V_DIM:(h + 1) * V_DIM, :],
                                (((0,), (0,)), ((), ())), preferred_element_type=F32)
    xn = x_ref[0] + gt_ref[0] * y
    if final:
        gf_ref, o_ref = rest
        ms = jnp.mean(xn * xn, axis=-1, keepdims=True)
        o_ref[0] = xn * lax.rsqrt(ms + EPS) * gf_ref[...]
    else:
        (o_ref,) = rest
        o_ref[0] = xn


def _outproj(ot, gat, g_sub, ybc, w, x, gt, g_final, *, lam_init, tm):
    b, s, d = x.shape
    tok = lambda width: pl.BlockSpec((1, tm, width), lambda i, j: (i, j, 0))
    heads_t = pl.BlockSpec((1, N_HEADS, V_DIM, tm), lambda i, j: (i, 0, 0, j))
    in_specs = [heads_t, heads_t, pl.BlockSpec((V_DIM, 1), lambda i, j: (0, 0)), tok(512),
                pl.BlockSpec((d, d), lambda i, j: (0, 0)), tok(d),
                pl.BlockSpec((1, 1, d), lambda i, j: (i, 0, 0))]
    args = [ot, gat, g_sub, ybc, w, x, gt]
    final = g_final is not None
    if final:
        in_specs.append(pl.BlockSpec((1, d), lambda i, j: (0, 0)))
        args.append(g_final.reshape(1, d))
    return pl.pallas_call(
        functools.partial(_outproj_kernel, final=final, lam_init=lam_init),
        grid=(b, s // tm),
        in_specs=in_specs,
        out_specs=tok(d),
        out_shape=jax.ShapeDtypeStruct((b, s, d), F32),
        compiler_params=_cparams(2),
        name="outproj",
    )(*args)


def _rope_tables(s):
    rows = s // GRID_W
    r = jnp.repeat(jnp.arange(rows, dtype=F32), GRID_W)
    col = jnp.tile(jnp.arange(GRID_W, dtype=F32), rows)
    half = HEAD_DIM // 2
    inv_freq = ROPE_BASE ** (-jnp.arange(0, half, 2, dtype=F32) / half)
    ang = jnp.concatenate([r[:, None] * inv_freq, col[:, None] * inv_freq], axis=-1)
    cos = jnp.repeat(jnp.cos(ang), 2, axis=-1)
    sin = jnp.repeat(jnp.sin(ang), 2, axis=-1)
    even = (jnp.arange(HEAD_DIM) % 2 == 0)[None, :]
    sa = jnp.where(even, -sin, 0.0)
    sb = jnp.where(even, 0.0, sin)
    return tuple(jnp.tile(t, (1, LANES // HEAD_DIM)) for t in (cos, sa, sb))


def kernel(x, c, ctx, c_ctx, w_ada, b_ada, g_norm, w_in, lam_q1, lam_k1, lam_q2, lam_k2,
           g_subln, w_short, w_conf, b_conf, g_conf_ln, b_conf_ln, w_out, g_final):
    b, s, d = x.shape
    n_ctx = ctx.shape[1]
    depth = w_in.shape[0]
    tm_x = min(512, s)
    tm_c = min(256, n_ctx)
    qb_x = min(512, s)
    qb_c = min(256, n_ctx)

    r = -(-(b + 1) // 8) * 8
    cond = jnp.zeros((r, d), F32).at[:b].set(c).at[b].set(c_ctx)
    mods = _modulation(cond, w_ada, b_ada)
    rope_tabs = _rope_tables(s)
    w_in_b = w_in.astype(BF16)
    w_out_b = w_out.astype(BF16)

    for l in range(depth):
        last = l == depth - 1
        lam_init = 0.8 - 0.6 * math.exp(-0.3 * l)
        lam_params = jnp.stack([lam_q1[l], lam_k1[l], lam_q2[l], lam_k2[l]]).astype(F32)
        g_sub = g_subln[l].reshape(V_DIM, 1)
        g_l = g_norm[l].reshape(1, d)
        m = mods[l]
        sh, sc, gt = m[:b, None, 0:d], m[:b, None, d:2 * d], m[:b, None, 2 * d:3 * d]
        bc = lambda v: jnp.broadcast_to(v[None, None, :], (b, 1, d))
        sh_c, sc_c, gt_c = bc(m[b, 0:d]), bc(m[b, d:2 * d]), bc(m[b, 2 * d:3 * d])
        conv_w = (w_short[l], w_conf[l], b_conf[l], g_conf_ln[l], b_conf_ln[l])

        qx, kx, vtx, gatx, cvx = _inproj(x, 1.0 + sc, sh, g_l, w_in_b[l], rope_tabs,
                                         kv_only=False, tm=tm_x)
        if not last:
            qc, kc, vtc, gatc, cvc = _inproj(ctx, 1.0 + sc_c, sh_c, g_l, w_in_b[l], None,
                                             kv_only=False, tm=tm_c)
        else:
            kc, vtc = _inproj(ctx, 1.0 + sc_c, sh_c, g_l, w_in_b[l][:, K_OFF:GA_OFF], None,
                              kv_only=True, tm=tm_c)

        ot = _attention(qx, lam_params, (kc, kx), (vtc, vtx), lam_init=lam_init, qb=qb_x)
        ybc = _conv_groups(cvx, *conv_w, t=min(256, s))
        x_new = _outproj(ot, gatx, g_sub, ybc, w_out_b[l], x, gt, g_final if last else None,
                         lam_init=lam_init, tm=tm_x)

        if not last:
            ot_c = _attention(qc, lam_params, (kc,), (vtc,), lam_init=lam_init, qb=qb_c)
            ybc_c = _conv_groups(cvc, *conv_w, t=min(256, n_ctx))
            ctx = _outproj(ot_c, gatc, g_sub, ybc_c, w_out_b[l], ctx, gt_c, None, lam_init=lam_init, tm=tm_c)
        x = x_new
    return x
```

```python
import functools
import math

import jax
import jax.numpy as jnp
from jax import lax
from jax.experimental import pallas as pl
from jax.experimental.pallas import tpu as pltpu

F32 = jnp.float32
BF16 = jnp.bfloat16

D_MODEL = 1024
GRID_W = 64
N_HEADS = 4
HEAD_DIM = 64
HEAD_QK = 2 * HEAD_DIM
V_DIM = 128
A_WIDTH = N_HEADS * V_DIM
B_WIDTH = 256
C_WIDTH = 256
SHORT_CONV = 3
CONF_CONV = 31
ROPE_BASE = 10000.0
EPS = 1e-6

Q_OFF = 0
K_OFF = 512
V_OFF = 1024
GA_OFF = 1536
B_OFF = 2048
C_OFF = 3072
D_IN = 3840

LANES = 128
CONV_HALO = 16
LOG2E = 1.4426950408889634
VMEM_LIMIT = 56 * 1024 * 1024


def _silu(t):
    return t * jax.nn.sigmoid(t)


def _cparams(n_axes):
    return pltpu.CompilerParams(dimension_semantics=("parallel",) * n_axes,
                                vmem_limit_bytes=VMEM_LIMIT)


def _mod_kernel(cond_ref, w_ref, b_ref, o_ref):
    a = _silu(cond_ref[...])
    o_ref[0] = jnp.dot(a.astype(BF16), w_ref[0].astype(BF16),
                       preferred_element_type=F32) + b_ref[0]


def _modulation(cond, w_ada, b_ada):
    depth, d, n = w_ada.shape
    r = cond.shape[0]
    bn = 1024
    return pl.pallas_call(
        _mod_kernel,
        grid=(depth, n // bn),
        in_specs=[pl.BlockSpec((r, d), lambda l, j: (0, 0)),
                  pl.BlockSpec((1, d, bn), lambda l, j: (l, 0, j)),
                  pl.BlockSpec((1, 1, bn), lambda l, j: (l, 0, j))],
        out_specs=pl.BlockSpec((1, r, bn), lambda l, j: (l, 0, j)),
        out_shape=jax.ShapeDtypeStruct((depth, r, n), F32),
        compiler_params=_cparams(2),
        name="modulation",
    )(cond, w_ada, b_ada.reshape(depth, 1, n))


def _rope(t, cos, sa, sb):
    return t * cos + pltpu.roll(t, LANES - 1, 1) * sa + pltpu.roll(t, 1, 1) * sb


def _store_heads_t(dst_ref, v):
    for h in range(N_HEADS):
        dst_ref[0, h] = v[:, h * V_DIM:(h + 1) * V_DIM].T.astype(BF16)


def _inproj_kernel(*refs, rope, kv_only, q_scale):
    x_ref, sc_ref, sh_ref, g_ref, w_ref = refs[:5]
    refs = refs[5:]
    if rope:
        cos_ref, sa_ref, sb_ref = refs[:3]
        refs = refs[3:]
        cos, sa, sb = cos_ref[...], sa_ref[...], sb_ref[...]

    x = x_ref[0]
    ms = jnp.mean(x * x, axis=-1, keepdims=True)
    h = x * lax.rsqrt(ms + EPS) * g_ref[...]
    hb = (h * sc_ref[0] + sh_ref[0]).astype(BF16)

    def proj(lo, hi):
        return jnp.dot(hb, w_ref[:, lo:hi], preferred_element_type=F32)

    def store_qk(dst_ref, t, scale):
        for j in range(t.shape[1] // LANES):
            tj = t[:, j * LANES:(j + 1) * LANES]
            if rope:
                tj = _rope(tj, cos, sa, sb)
            if scale != 1.0:
                tj = tj * scale
            dst_ref[0, :, j * LANES:(j + 1) * LANES] = tj.astype(BF16)

    if kv_only:
        k_ref, vt_ref = refs
        store_qk(k_ref, proj(0, 512), 1.0)
        _store_heads_t(vt_ref, proj(512, 1024))
        return

    q_ref, k_ref, vt_ref, gat_ref, cv_ref = refs
    store_qk(q_ref, proj(Q_OFF, K_OFF), q_scale)
    store_qk(k_ref, proj(K_OFF, V_OFF), 1.0)
    _store_heads_t(vt_ref, proj(V_OFF, GA_OFF))
    _store_heads_t(gat_ref, proj(GA_OFF, B_OFF))

    pb = proj(B_OFF, C_OFF)
    bb, bc = pb[:, 0:256], pb[:, 256:512]
    bh, gb = pb[:, 512:768], pb[:, 768:1024]
    cv_ref[0, :, 0:256] = (bb * _silu(gb)).astype(BF16)
    cv_ref[0, :, 256:512] = (bc * bh).astype(BF16)
    pc = proj(C_OFF, D_IN)
    ca, cb, gc = pc[:, 0:256], pc[:, 256:512], pc[:, 512:768]
    cv_ref[0, :, 512:768] = (ca * jax.nn.sigmoid(cb)).astype(BF16)
    cv_ref[0, :, 768:1024] = _silu(gc).astype(BF16)


def _inproj(x, sc1p, sh, g, w, rope_tabs, *, kv_only, tm):
    b, s, d = x.shape
    rope = rope_tabs is not None
    n = w.shape[1]
    grid = (b, s // tm)
    in_specs = [pl.BlockSpec((1, tm, d), lambda i, j: (i, j, 0)),
                pl.BlockSpec((1, 1, d), lambda i, j: (i, 0, 0)),
                pl.BlockSpec((1, 1, d), lambda i, j: (i, 0, 0)),
                pl.BlockSpec((1, d), lambda i, j: (0, 0)),
                pl.BlockSpec((d, n), lambda i, j: (0, 0))]
    args = [x, sc1p, sh, g, w]
    if rope:
        in_specs += [pl.BlockSpec((tm, LANES), lambda i, j: (j, 0))] * 3
        args += list(rope_tabs)
    tok = lambda width: (jax.ShapeDtypeStruct((b, s, width), BF16),
                         pl.BlockSpec((1, tm, width), lambda i, j: (i, j, 0)))
    heads_t = (jax.ShapeDtypeStruct((b, N_HEADS, V_DIM, s), BF16),
               pl.BlockSpec((1, N_HEADS, V_DIM, tm), lambda i, j: (i, 0, 0, j)))
    outs = [tok(512), heads_t] if kv_only else [tok(512), tok(512), heads_t, heads_t, tok(1024)]
    q_scale = LOG2E * HEAD_DIM ** -0.5
    return pl.pallas_call(
        functools.partial(_inproj_kernel, rope=rope, kv_only=kv_only, q_scale=q_scale),
        grid=grid,
        in_specs=in_specs,
        out_specs=[o[1] for o in outs],
        out_shape=[o[0] for o in outs],
        compiler_params=_cparams(2),
        name="inproj",
    )(*args)


def _attn_kernel(*refs, seg_lens, chunk, lam_init):
    n_seg = len(seg_lens)
    q_ref, lamp_ref = refs[:2]
    k_refs = refs[2:2 + n_seg]
    vt_refs = refs[2 + n_seg:2 + 2 * n_seg]
    o_ref = refs[2 + 2 * n_seg]
    even, odd = refs[3 + 2 * n_seg:7 + 2 * n_seg], refs[7 + 2 * n_seg:]
    t = pl.program_id(0)

    @pl.when(t == 0)
    def _():
        s_odd, mx_odd, acc_odd, den_odd = odd
        s_odd[...] = jnp.zeros(s_odd.shape, F32)
        mx_odd[...] = jnp.zeros(mx_odd.shape, F32)
        acc_odd[...] = jnp.zeros(acc_odd.shape, F32)
        den_odd[...] = jnp.ones(den_odd.shape, F32)

    chunks = []
    off = 0
    for si, n in enumerate(seg_lens):
        for c0 in range(0, n, chunk):
            chunks.append((si, c0, min(chunk, n - c0), off + c0))
        off += n

    def step(wr, rd):
        s_wr, mx_wr, acc_wr, den_wr = wr
        s_rd, mx_rd, acc_rd, den_rd = rd

        lp = lamp_ref[...]
        lam = (jnp.exp(jnp.sum(lp[0:1] * lp[1:2], axis=-1, keepdims=True))
               - jnp.exp(jnp.sum(lp[2:3] * lp[3:4], axis=-1, keepdims=True)) + lam_init)
        r0 = 1.0 / den_rd[0][0:1]
        r1 = lam / den_rd[1][0:1]
        o_ref[0, 0] = (acc_rd[0] * r0 - acc_rd[1] * r1).astype(BF16)

        q = q_ref[0]
        lane = lax.broadcasted_iota(jnp.int32, q.shape, 1)
        zero = jnp.zeros_like(q)
        q_maps = (jnp.where(lane < HEAD_DIM, q, zero), jnp.where(lane >= HEAD_DIM, q, zero))
        for m in range(2):
            mx_new = None
            mx_old = mx_rd[m][0:1]
            a = None
            den = None
            for si, c0, cn, r0 in chunks:
                s = lax.dot_general(k_refs[si][0, c0:c0 + cn, :], q_maps[m],
                                    (((1,), (1,)), ((), ())), preferred_element_type=F32)
                s_wr[m, r0:r0 + cn, :] = s
                cm = jnp.max(s, axis=0, keepdims=True)
                mx_new = cm if mx_new is None else jnp.maximum(mx_new, cm)

                e = jnp.exp2(s_rd[m, r0:r0 + cn, :] - mx_old)
                ds = jnp.sum(e, axis=0, keepdims=True)
                den = ds if den is None else den + ds
                pv = jnp.dot(vt_refs[si][0, 0, :, c0:c0 + cn], e.astype(BF16), preferred_element_type=F32)
                a = pv if a is None else a + pv
            mx_wr[m] = jnp.broadcast_to(mx_new, mx_wr.shape[1:])
            den_wr[m] = jnp.broadcast_to(den, den_wr.shape[1:])
            acc_wr[m] = a

    @pl.when(t % 2 == 0)
    def _():
        step(even, odd)

    @pl.when(t % 2 == 1)
    def _():
        step(odd, even)


def _attention(q, lam_params, ks, vts, *, lam_init, qb):
    b, s, _ = q.shape
    seg_lens = tuple(k.shape[1] for k in ks)
    ktot = sum(seg_lens)
    nq = s // qb
    n_units = b * N_HEADS * nq

    def unit(t, lag):
        u = jnp.clip(t - lag, 0, n_units - 1)
        return u // (N_HEADS * nq), (u // nq) % N_HEADS, u % nq

    def q_a(t):
        i, h, j = unit(t, 0)
        return i, j, h

    def out_c(t):
        i, h, j = unit(t, 2)
        return i, h, 0, j

    def keys_a(t):
        i, h, _ = unit(t, 0)
        return i, 0, h

    def vals_b(t):
        i, h, _ = unit(t, 1)
        return i, h, 0, 0

    in_specs = [pl.BlockSpec((1, qb, HEAD_QK), q_a),
                pl.BlockSpec((4, HEAD_DIM), lambda t: (0, 0))]
    in_specs += [pl.BlockSpec((1, n, HEAD_QK), keys_a) for n in seg_lens]
    in_specs += [pl.BlockSpec((1, 1, V_DIM, n), vals_b) for n in seg_lens]
    parity_scratch = [pltpu.VMEM((2, ktot, qb), F32), pltpu.VMEM((2, 8, qb), F32),
                      pltpu.VMEM((2, V_DIM, qb), F32), pltpu.VMEM((2, 8, qb), F32)]
    return pl.pallas_call(
        functools.partial(_attn_kernel, seg_lens=seg_lens, chunk=512, lam_init=lam_init),
        grid=(n_units + 2,),
        in_specs=in_specs,
        out_specs=pl.BlockSpec((1, 1, V_DIM, qb), out_c),
        out_shape=jax.ShapeDtypeStruct((b, N_HEADS, V_DIM, s), BF16),
        scratch_shapes=parity_scratch * 2,
        compiler_params=pltpu.CompilerParams(dimension_semantics=("arbitrary",),
                                             vmem_limit_bytes=VMEM_LIMIT),
        name="attention",
    )(q, lam_params, *ks, *vts)


def _taps(win, t, taps):
    rows = win.shape[0]
    acc = None
    for b in range(8):
        group = [(off, w) for off, w in taps if off % 8 == b]
        if not group:
            continue
        wb = win if b == 0 else pltpu.roll(win, rows - b, 0)
        for off, w in group:
            term = wb[off - b:off - b + t] * w
            acc = term if acc is None else acc + term
    return acc


def _conv_fill(cv_ref, ub_scr, uc_scr, s):
    zeros = jnp.zeros((CONV_HALO, B_WIDTH), F32)
    for scr, lo in ((ub_scr, 256), (uc_scr, 512)):
        scr[0:CONV_HALO, :] = zeros
        scr[CONV_HALO + s:2 * CONV_HALO + s, :] = zeros
        scr[CONV_HALO:CONV_HALO + s, :] = cv_ref[0, :, lo:lo + 256].astype(F32)


def _conv_groups(cv_ref, ub_scr, uc_scr, conv_refs, base, t):
    wsh_ref, wcf_ref, bcf_ref, gln_ref, bln_ref = conv_refs
    half = SHORT_CONV // 2
    win_b = ub_scr[pl.ds(base + 8, t + 16), :]
    conv_b = _taps(win_b, t, [(8 - half + k, wsh_ref[k:k + 1, :]) for k in range(SHORT_CONV)])
    half = CONF_CONV // 2
    win_c = uc_scr[pl.ds(base, t + 32), :]
    conv_c = _taps(win_c, t, [(CONV_HALO - half + k, wcf_ref[k:k + 1, :]) for k in range(CONF_CONV)])

    tok = pl.ds(base, t)
    yb = cv_ref[0, tok, 0:256].astype(F32) * conv_b
    u = conv_c + bcf_ref[...]
    mu = jnp.mean(u, axis=-1, keepdims=True)
    uc = u - mu
    var = jnp.mean(uc * uc, axis=-1, keepdims=True)
    ln = uc * lax.rsqrt(var + EPS) * gln_ref[...] + bln_ref[...]
    yc = _silu(ln) * cv_ref[0, tok, 768:1024].astype(F32)
    return yb.astype(BF16), yc.astype(BF16)


def _outproj_kernel(*refs, final, lam_init, s, tm, conv_rows):
    ot_ref, gat_ref, gsub_ref, cv_ref = refs[:4]
    conv_refs = refs[4:9]
    w_ref, x_ref, gt_ref = refs[9:12]
    if final:
        gf_ref, o_ref, ub_scr, uc_scr = refs[12:]
    else:
        o_ref, ub_scr, uc_scr = refs[12:]
    c = pl.program_id(1)

    @pl.when(c == 0)
    def _():
        _conv_fill(cv_ref, ub_scr, uc_scr, s)

    g = gsub_ref[...] * (1.0 - lam_init)
    ya = []
    for h in range(N_HEADS):
        o = ot_ref[0, h].astype(F32)
        ms = jnp.mean(o * o, axis=0, keepdims=True)
        ya.append((o * lax.rsqrt(ms + EPS) * g * _silu(gat_ref[0, h].astype(F32))).astype(BF16))

    for r0 in range(0, tm, conv_rows):
        rows = slice(r0, r0 + conv_rows)
        yb, yc = _conv_groups(cv_ref, ub_scr, uc_scr, conv_refs, pl.multiple_of(c * tm + r0, 8), conv_rows)
        y = (jnp.dot(yb, w_ref[A_WIDTH:A_WIDTH + B_WIDTH, :], preferred_element_type=F32)
             + jnp.dot(yc, w_ref[A_WIDTH + B_WIDTH:, :], preferred_element_type=F32))
        for h in range(N_HEADS):
            y = y + lax.dot_general(ya[h][:, rows], w_ref[h * V_DIM:(h + 1) * V_DIM, :],
                                    (((0,), (0,)), ((), ())), preferred_element_type=F32)
        xn = x_ref[0, rows, :] + gt_ref[0] * y
        if final:
            ms = jnp.mean(xn * xn, axis=-1, keepdims=True)
            o_ref[0, rows, :] = xn * lax.rsqrt(ms + EPS) * gf_ref[...]
        else:
            o_ref[0, rows, :] = xn


def _outproj(ot, gat, g_sub, cv, conv_w, w, x, gt, g_final, *, lam_init, tm):
    b, s, d = x.shape
    w_short, w_conf, b_conf, g_ln, b_ln = conv_w
    tok = lambda width: pl.BlockSpec((1, tm, width), lambda i, j: (i, j, 0))
    heads_t = pl.BlockSpec((1, N_HEADS, V_DIM, tm), lambda i, j: (i, 0, 0, j))
    vec = lambda rows: pl.BlockSpec((rows, 256), lambda i, j: (0, 0))
    in_specs = [heads_t, heads_t, pl.BlockSpec((V_DIM, 1), lambda i, j: (0, 0)),
                pl.BlockSpec((1, s, 1024), lambda i, j: (i, 0, 0)),
                vec(SHORT_CONV), vec(CONF_CONV), vec(1), vec(1), vec(1),
                pl.BlockSpec((d, d), lambda i, j: (0, 0)), tok(d),
                pl.BlockSpec((1, 1, d), lambda i, j: (i, 0, 0))]
    args = [ot, gat, g_sub, cv, w_short, w_conf, b_conf.reshape(1, -1), g_ln.reshape(1, -1),
            b_ln.reshape(1, -1), w, x, gt]
    final = g_final is not None
    if final:
        in_specs.append(pl.BlockSpec((1, d), lambda i, j: (0, 0)))
        args.append(g_final.reshape(1, d))
    return pl.pallas_call(
        functools.partial(_outproj_kernel, final=final, lam_init=lam_init, s=s, tm=tm,
                          conv_rows=min(256, tm)),
        grid=(b, s // tm),
        in_specs=in_specs,
        out_specs=tok(d),
        out_shape=jax.ShapeDtypeStruct((b, s, d), F32),
        scratch_shapes=[pltpu.VMEM((s + 2 * CONV_HALO, B_WIDTH), F32),
                        pltpu.VMEM((s + 2 * CONV_HALO, C_WIDTH), F32)],
        compiler_params=pltpu.CompilerParams(dimension_semantics=("parallel", "arbitrary"),
                                             vmem_limit_bytes=VMEM_LIMIT),
        name="outproj",
    )(*args)


def _rope_tables(s):
    rows = s // GRID_W
    r = jnp.repeat(jnp.arange(rows, dtype=F32), GRID_W)
    col = jnp.tile(jnp.arange(GRID_W, dtype=F32), rows)
    half = HEAD_DIM // 2
    inv_freq = ROPE_BASE ** (-jnp.arange(0, half, 2, dtype=F32) / half)
    ang = jnp.concatenate([r[:, None] * inv_freq, col[:, None] * inv_freq], axis=-1)
    cos = jnp.repeat(jnp.cos(ang), 2, axis=-1)
    sin = jnp.repeat(jnp.sin(ang), 2, axis=-1)
    even = (jnp.arange(HEAD_DIM) % 2 == 0)[None, :]
    sa = jnp.where(even, -sin, 0.0)
    sb = jnp.where(even, 0.0, sin)
    return tuple(jnp.tile(t, (1, LANES // HEAD_DIM)) for t in (cos, sa, sb))


def kernel(x, c, ctx, c_ctx, w_ada, b_ada, g_norm, w_in, lam_q1, lam_k1, lam_q2, lam_k2,
           g_subln, w_short, w_conf, b_conf, g_conf_ln, b_conf_ln, w_out, g_final):
    b, s, d = x.shape
    n_ctx = ctx.shape[1]
    depth = w_in.shape[0]
    tm_x = min(512, s)
    tm_c = min(256, n_ctx)
    qb_x = min(512, s)
    qb_c = min(256, n_ctx)

    r = -(-(b + 1) // 8) * 8
    cond = jnp.zeros((r, d), F32).at[:b].set(c).at[b].set(c_ctx)
    mods = _modulation(cond, w_ada, b_ada)
    rope_tabs = _rope_tables(s)
    w_in_b = w_in.astype(BF16)
    w_out_b = w_out.astype(BF16)

    for l in range(depth):
        last = l == depth - 1
        lam_init = 0.8 - 0.6 * math.exp(-0.3 * l)
        lam_params = jnp.stack([lam_q1[l], lam_k1[l], lam_q2[l], lam_k2[l]]).astype(F32)
        g_sub = g_subln[l].reshape(V_DIM, 1)
        g_l = g_norm[l].reshape(1, d)
        m = mods[l]
        sh, sc, gt = m[:b, None, 0:d], m[:b, None, d:2 * d], m[:b, None, 2 * d:3 * d]
        bc = lambda v: jnp.broadcast_to(v[None, None, :], (b, 1, d))
        sh_c, sc_c, gt_c = bc(m[b, 0:d]), bc(m[b, d:2 * d]), bc(m[b, 2 * d:3 * d])
        conv_w = (w_short[l], w_conf[l], b_conf[l], g_conf_ln[l], b_conf_ln[l])

        qx, kx, vtx, gatx, cvx = _inproj(x, 1.0 + sc, sh, g_l, w_in_b[l], rope_tabs,
                                         kv_only=False, tm=tm_x)
        if not last:
            qc, kc, vtc, gatc, cvc = _inproj(ctx, 1.0 + sc_c, sh_c, g_l, w_in_b[l], None,
                                             kv_only=False, tm=tm_c)
        else:
            kc, vtc = _inproj(ctx, 1.0 + sc_c, sh_c, g_l, w_in_b[l][:, K_OFF:GA_OFF], None,
                              kv_only=True, tm=tm_c)

        ot = _attention(qx, lam_params, (kc, kx), (vtc, vtx), lam_init=lam_init, qb=qb_x)
        x_new = _outproj(ot, gatx, g_sub, cvx, conv_w, w_out_b[l], x, gt, g_final if last else None,
                         lam_init=lam_init, tm=tm_x)

        if not last:
            ot_c = _attention(qc, lam_params, (kc,), (vtc,), lam_init=lam_init, qb=qb_c)
            ctx = _outproj(ot_c, gatc, g_sub, cvc, conv_w, w_out_b[l], ctx, gt_c, None,
                           lam_init=lam_init, tm=tm_c)
        x = x_new
    return x
```

```python
import functools
import math

import jax
import jax.numpy as jnp
from jax import lax
from jax.experimental import pallas as pl
from jax.experimental.pallas import tpu as pltpu

F32 = jnp.float32
BF16 = jnp.bfloat16

D_MODEL = 1024
GRID_W = 64
N_HEADS = 4
HEAD_DIM = 64
HEAD_QK = 2 * HEAD_DIM
V_DIM = 128
A_WIDTH = N_HEADS * V_DIM
B_WIDTH = 256
C_WIDTH = 256
SHORT_CONV = 3
CONF_CONV = 31
ROPE_BASE = 10000.0
EPS = 1e-6

Q_OFF = 0
K_OFF = 512
V_OFF = 1024
GA_OFF = 1536
B_OFF = 2048
C_OFF = 3072
D_IN = 3840

LANES = 128
CONV_HALO = 16
LOG2E = 1.4426950408889634
VMEM_LIMIT = 56 * 1024 * 1024


def _silu(t):
    return t * jax.nn.sigmoid(t)


def _cparams(n_axes):
    return pltpu.CompilerParams(dimension_semantics=("parallel",) * n_axes,
                                vmem_limit_bytes=VMEM_LIMIT)


def _mod_kernel(cond_ref, w_ref, b_ref, o_ref):
    a = _silu(cond_ref[...])
    o_ref[0] = jnp.dot(a.astype(BF16), w_ref[0].astype(BF16),
                       preferred_element_type=F32) + b_ref[0]


def _modulation(cond, w_ada, b_ada):
    depth, d, n = w_ada.shape
    r = cond.shape[0]
    bn = 1024
    return pl.pallas_call(
        _mod_kernel,
        grid=(depth, n // bn),
        in_specs=[pl.BlockSpec((r, d), lambda l, j: (0, 0)),
                  pl.BlockSpec((1, d, bn), lambda l, j: (l, 0, j)),
                  pl.BlockSpec((1, 1, bn), lambda l, j: (l, 0, j))],
        out_specs=pl.BlockSpec((1, r, bn), lambda l, j: (l, 0, j)),
        out_shape=jax.ShapeDtypeStruct((depth, r, n), F32),
        compiler_params=_cparams(2),
        name="modulation",
    )(cond, w_ada, b_ada.reshape(depth, 1, n))


def _taps(win, t, taps):
    rows = win.shape[0]
    acc = None
    for b in range(8):
        group = [(off, w) for off, w in taps if off % 8 == b]
        if not group:
            continue
        wb = win if b == 0 else pltpu.roll(win, rows - b, 0)
        for off, w in group:
            term = wb[off - b:off - b + t] * w
            acc = term if acc is None else acc + term
    return acc


def _rope(t, cos, sa, sb):
    return t * cos + pltpu.roll(t, LANES - 1, 1) * sa + pltpu.roll(t, 1, 1) * sb


def _store_heads_t(dst_ref, v):
    for h in range(N_HEADS):
        dst_ref[0, h] = v[:, h * V_DIM:(h + 1) * V_DIM].T.astype(BF16)


def _inproj_kernel(*refs, rope, kv_only, q_scale, conv_rows):
    x_ref, sc_ref, sh_ref, g_ref, w_ref = refs[:5]
    refs = refs[5:]
    if rope:
        cos_ref, sa_ref, sb_ref = refs[:3]
        refs = refs[3:]
        cos, sa, sb = cos_ref[...], sa_ref[...], sb_ref[...]

    def normed(x):
        ms = jnp.mean(x * x, axis=-1, keepdims=True)
        h = x * lax.rsqrt(ms + EPS) * g_ref[...]
        return (h * sc_ref[0] + sh_ref[0]).astype(BF16)

    hb = normed(x_ref[0])

    def proj(lo, hi):
        return jnp.dot(hb, w_ref[:, lo:hi], preferred_element_type=F32)

    def roped(t, h):
        th = t[:, h * HEAD_QK:(h + 1) * HEAD_QK]
        return _rope(th, cos, sa, sb) if rope else th

    def store_k(dst_ref, t):
        for h in range(N_HEADS):
            dst_ref[0, :, h * HEAD_QK:(h + 1) * HEAD_QK] = roped(t, h).astype(BF16)

    def store_qt(dst_ref, t, scale):
        for h in range(N_HEADS):
            dst_ref[0, h] = (roped(t, h) * scale).T.astype(BF16)

    if kv_only:
        k_ref, vt_ref = refs
        store_k(k_ref, proj(0, 512))
        _store_heads_t(vt_ref, proj(512, 1024))
        return

    xp_ref, xn_ref, wsh_ref, wcf_ref, bcf_ref, gln_ref, bln_ref = refs[:7]
    qt_ref, k_ref, vt_ref, gat_ref, ybc_ref = refs[7:]
    tm = hb.shape[0]

    hb_ext = jnp.concatenate([normed(xp_ref[0, 0]), hb, normed(xn_ref[0, 0])], axis=0)
    pb = jnp.dot(hb_ext, w_ref[:, B_OFF:C_OFF], preferred_element_type=F32)
    pc = jnp.dot(hb_ext, w_ref[:, C_OFF:D_IN], preferred_element_type=F32)
    j = pl.program_id(1)
    row = lax.broadcasted_iota(jnp.int32, (tm + 2 * CONV_HALO, 1), 0)
    inside = jnp.logical_and(jnp.logical_or(row >= CONV_HALO, j > 0),
                             jnp.logical_or(row < tm + CONV_HALO, j < pl.num_programs(1) - 1))
    keep = jnp.where(inside, 1.0, 0.0)
    ub = pb[:, 256:512] * pb[:, 512:768] * keep
    uc = pc[:, 0:256] * jax.nn.sigmoid(pc[:, 256:512]) * keep
    body = slice(CONV_HALO, CONV_HALO + tm)
    gate_b = pb[body, 0:256] * _silu(pb[body, 768:1024])
    gate_c = _silu(pc[body, 512:768])
    others = [lambda: store_qt(qt_ref, proj(Q_OFF, K_OFF), q_scale),
              lambda: store_k(k_ref, proj(K_OFF, V_OFF)),
              lambda: _store_heads_t(vt_ref, proj(V_OFF, GA_OFF)),
              lambda: _store_heads_t(gat_ref, proj(GA_OFF, B_OFF))]
    n_blocks = tm // conv_rows
    per_block = -(-len(others) // n_blocks)
    for r0 in range(0, tm, conv_rows):
        rows = slice(r0, r0 + conv_rows)
        half = SHORT_CONV // 2
        conv_b = _taps(ub[r0 + 8:r0 + conv_rows + 24], conv_rows,
                       [(8 - half + k, wsh_ref[k:k + 1, :]) for k in range(SHORT_CONV)])
        half = CONF_CONV // 2
        conv_c = _taps(uc[r0:r0 + conv_rows + 2 * CONV_HALO], conv_rows,
                       [(CONV_HALO - half + k, wcf_ref[k:k + 1, :]) for k in range(CONF_CONV)])
        u = conv_c + bcf_ref[...]
        mu = jnp.mean(u, axis=-1, keepdims=True)
        uz = u - mu
        var = jnp.mean(uz * uz, axis=-1, keepdims=True)
        ln = uz * lax.rsqrt(var + EPS) * gln_ref[...] + bln_ref[...]
        ybc_ref[0, rows, 0:256] = (gate_b[rows] * conv_b).astype(BF16)
        ybc_ref[0, rows, 256:512] = (_silu(ln) * gate_c[rows]).astype(BF16)
        for emit in others[:per_block]:
            emit()
        others = others[per_block:]


def _inproj(x, sc1p, sh, g, w, rope_tabs, conv_w, *, kv_only, tm):
    b, s, d = x.shape
    rope = rope_tabs is not None
    n = w.shape[1]
    grid = (b, s // tm)
    in_specs = [pl.BlockSpec((1, tm, d), lambda i, j: (i, j, 0)),
                pl.BlockSpec((1, 1, d), lambda i, j: (i, 0, 0)),
                pl.BlockSpec((1, 1, d), lambda i, j: (i, 0, 0)),
                pl.BlockSpec((1, d), lambda i, j: (0, 0)),
                pl.BlockSpec((d, n), lambda i, j: (0, 0))]
    args = [x, sc1p, sh, g, w]
    if rope:
        in_specs += [pl.BlockSpec((tm, LANES), lambda i, j: (j, 0))] * 3
        args += list(rope_tabs)
    if not kv_only:
        w_short, w_conf, b_conf, g_ln, b_ln = conv_w
        hb_per_tile = tm // CONV_HALO
        n_hb = s // CONV_HALO
        halo = lambda index: pl.BlockSpec((1, 1, CONV_HALO, d), index)
        vec = lambda rows: pl.BlockSpec((rows, 256), lambda i, j: (0, 0))
        x_blocks = x.reshape(b, n_hb, CONV_HALO, d)
        in_specs += [halo(lambda i, j: (i, jnp.maximum(j * hb_per_tile - 1, 0), 0, 0)),
                     halo(lambda i, j: (i, jnp.minimum((j + 1) * hb_per_tile, n_hb - 1), 0, 0)),
                     vec(SHORT_CONV), vec(CONF_CONV), vec(1), vec(1), vec(1)]
        args += [x_blocks, x_blocks, w_short, w_conf, b_conf.reshape(1, -1), g_ln.reshape(1, -1),
                 b_ln.reshape(1, -1)]
    tok = lambda width: (jax.ShapeDtypeStruct((b, s, width), BF16),
                         pl.BlockSpec((1, tm, width), lambda i, j: (i, j, 0)))
    heads_t = (jax.ShapeDtypeStruct((b, N_HEADS, V_DIM, s), BF16),
               pl.BlockSpec((1, N_HEADS, V_DIM, tm), lambda i, j: (i, 0, 0, j)))
    outs = [tok(512), heads_t] if kv_only else [heads_t, tok(512), heads_t, heads_t, tok(512)]
    q_scale = LOG2E * HEAD_DIM ** -0.5
    return pl.pallas_call(
        functools.partial(_inproj_kernel, rope=rope, kv_only=kv_only, q_scale=q_scale,
                          conv_rows=min(256, tm)),
        grid=grid,
        in_specs=in_specs,
        out_specs=[o[1] for o in outs],
        out_shape=[o[0] for o in outs],
        compiler_params=_cparams(2),
        name="inproj",
    )(*args)


def _attn_kernel(*refs, seg_lens, chunk, lam_init):
    n_seg = len(seg_lens)
    q_ref, lamp_ref = refs[:2]
    k_refs = refs[2:2 + n_seg]
    vt_refs = refs[2 + n_seg:2 + 2 * n_seg]
    o_ref = refs[2 + 2 * n_seg]
    even, odd = refs[3 + 2 * n_seg:7 + 2 * n_seg], refs[7 + 2 * n_seg:]
    t = pl.program_id(0)

    @pl.when(t == 0)
    def _():
        s_odd, mx_odd, acc_odd, den_odd = odd
        s_odd[...] = jnp.zeros(s_odd.shape, F32)
        mx_odd[...] = jnp.zeros(mx_odd.shape, F32)
        acc_odd[...] = jnp.zeros(acc_odd.shape, F32)
        den_odd[...] = jnp.ones(den_odd.shape, F32)

    chunks = []
    off = 0
    for si, n in enumerate(seg_lens):
        for c0 in range(0, n, chunk):
            chunks.append((si, c0, min(chunk, n - c0), off + c0))
        off += n

    def step(wr, rd):
        s_wr, mx_wr, acc_wr, den_wr = wr
        s_rd, mx_rd, acc_rd, den_rd = rd

        lp = lamp_ref[...]
        lam = (jnp.exp(jnp.sum(lp[0:1] * lp[1:2], axis=-1, keepdims=True))
               - jnp.exp(jnp.sum(lp[2:3] * lp[3:4], axis=-1, keepdims=True)) + lam_init)
        r0 = 1.0 / den_rd[0][0:1]
        r1 = lam / den_rd[1][0:1]
        o_ref[0, 0] = (acc_rd[0] * r0 - acc_rd[1] * r1).astype(BF16)

        qt = q_ref[0, 0]
        dim = lax.broadcasted_iota(jnp.int32, qt.shape, 0)
        zero = jnp.zeros_like(qt)
        q_maps = (jnp.where(dim < HEAD_DIM, qt, zero), jnp.where(dim >= HEAD_DIM, qt, zero))
        for m in range(2):
            mx_new = None
            mx_old = mx_rd[m][0:1]
            a = None
            den = None
            for si, c0, cn, r0 in chunks:
                s = jnp.dot(k_refs[si][0, c0:c0 + cn, :], q_maps[m], preferred_element_type=F32)
                s_wr[m, r0:r0 + cn, :] = s
                cm = jnp.max(s, axis=0, keepdims=True)
                mx_new = cm if mx_new is None else jnp.maximum(mx_new, cm)

                e = jnp.exp2(s_rd[m, r0:r0 + cn, :] - mx_old)
                ds = jnp.sum(e, axis=0, keepdims=True)
                den = ds if den is None else den + ds
                pv = jnp.dot(vt_refs[si][0, 0, :, c0:c0 + cn], e.astype(BF16), preferred_element_type=F32)
                a = pv if a is None else a + pv
            mx_wr[m] = jnp.broadcast_to(mx_new, mx_wr.shape[1:])
            den_wr[m] = jnp.broadcast_to(den, den_wr.shape[1:])
            acc_wr[m] = a

    @pl.when(t % 2 == 0)
    def _():
        step(even, odd)

    @pl.when(t % 2 == 1)
    def _():
        step(odd, even)


def _attention(q, lam_params, ks, vts, *, lam_init, qb):
    b, _, _, s = q.shape
    seg_lens = tuple(k.shape[1] for k in ks)
    ktot = sum(seg_lens)
    nq = s // qb
    n_units = b * N_HEADS * nq

    def unit(t, lag):
        u = jnp.clip(t - lag, 0, n_units - 1)
        return u // (N_HEADS * nq), (u // nq) % N_HEADS, u % nq

    def q_a(t):
        i, h, j = unit(t, 0)
        return i, h, 0, j

    def out_c(t):
        i, h, j = unit(t, 2)
        return i, h, 0, j

    def keys_a(t):
        i, h, _ = unit(t, 0)
        return i, 0, h

    def vals_b(t):
        i, h, _ = unit(t, 1)
        return i, h, 0, 0

    in_specs = [pl.BlockSpec((1, 1, HEAD_QK, qb), q_a),
                pl.BlockSpec((4, HEAD_DIM), lambda t: (0, 0))]
    in_specs += [pl.BlockSpec((1, n, HEAD_QK), keys_a) for n in seg_lens]
    in_specs += [pl.BlockSpec((1, 1, V_DIM, n), vals_b) for n in seg_lens]
    parity_scratch = [pltpu.VMEM((2, ktot, qb), F32), pltpu.VMEM((2, 8, qb), F32),
                      pltpu.VMEM((2, V_DIM, qb), F32), pltpu.VMEM((2, 8, qb), F32)]
    return pl.pallas_call(
        functools.partial(_attn_kernel, seg_lens=seg_lens, chunk=512, lam_init=lam_init),
        grid=(n_units + 2,),
        in_specs=in_specs,
        out_specs=pl.BlockSpec((1, 1, V_DIM, qb), out_c),
        out_shape=jax.ShapeDtypeStruct((b, N_HEADS, V_DIM, s), BF16),
        scratch_shapes=parity_scratch * 2,
        compiler_params=pltpu.CompilerParams(dimension_semantics=("arbitrary",),
                                             vmem_limit_bytes=VMEM_LIMIT),
        name="attention",
    )(q, lam_params, *ks, *vts)


def _outproj_kernel(ot_ref, gat_ref, gsub_ref, ybc_ref, w_ref, x_ref, gt_ref, *rest, final, lam_init):
    y = jnp.dot(ybc_ref[0], w_ref[A_WIDTH:, :], preferred_element_type=F32)
    g = gsub_ref[...] * (1.0 - lam_init)
    for h in range(N_HEADS):
        o = ot_ref[0, h].astype(F32)
        ms = jnp.mean(o * o, axis=0, keepdims=True)
        ya = o * lax.rsqrt(ms + EPS) * g * _silu(gat_ref[0, h].astype(F32))
        y = y + lax.dot_general(ya.astype(BF16), w_ref[h * V_DIM:(h + 1) * V_DIM, :],
                                (((0,), (0,)), ((), ())), preferred_element_type=F32)
    xn = x_ref[0] + gt_ref[0] * y
    if final:
        gf_ref, o_ref = rest
        ms = jnp.mean(xn * xn, axis=-1, keepdims=True)
        o_ref[0] = xn * lax.rsqrt(ms + EPS) * gf_ref[...]
    else:
        (o_ref,) = rest
        o_ref[0] = xn


def _outproj(ot, gat, g_sub, ybc, w, x, gt, g_final, *, lam_init, tm):
    b, s, d = x.shape
    tok = lambda width: pl.BlockSpec((1, tm, width), lambda i, j: (i, j, 0))
    heads_t = pl.BlockSpec((1, N_HEADS, V_DIM, tm), lambda i, j: (i, 0, 0, j))
    in_specs = [heads_t, heads_t, pl.BlockSpec((V_DIM, 1), lambda i, j: (0, 0)), tok(512),
                pl.BlockSpec((d, d), lambda i, j: (0, 0)), tok(d),
                pl.BlockSpec((1, 1, d), lambda i, j: (i, 0, 0))]
    args = [ot, gat, g_sub, ybc, w, x, gt]
    final = g_final is not None
    if final:
        in_specs.append(pl.BlockSpec((1, d), lambda i, j: (0, 0)))
        args.append(g_final.reshape(1, d))
    return pl.pallas_call(
        functools.partial(_outproj_kernel, final=final, lam_init=lam_init),
        grid=(b, s // tm),
        in_specs=in_specs,
        out_specs=tok(d),
        out_shape=jax.ShapeDtypeStruct((b, s, d), F32),
        compiler_params=_cparams(2),
        name="outproj",
    )(*args)


def _rope_tables(s):
    rows = s // GRID_W
    r = jnp.repeat(jnp.arange(rows, dtype=F32), GRID_W)
    col = jnp.tile(jnp.arange(GRID_W, dtype=F32), rows)
    half = HEAD_DIM // 2
    inv_freq = ROPE_BASE ** (-jnp.arange(0, half, 2, dtype=F32) / half)
    ang = jnp.concatenate([r[:, None] * inv_freq, col[:, None] * inv_freq], axis=-1)
    cos = jnp.repeat(jnp.cos(ang), 2, axis=-1)
    sin = jnp.repeat(jnp.sin(ang), 2, axis=-1)
    even = (jnp.arange(HEAD_DIM) % 2 == 0)[None, :]
    sa = jnp.where(even, -sin, 0.0)
    sb = jnp.where(even, 0.0, sin)
    return tuple(jnp.tile(t, (1, LANES // HEAD_DIM)) for t in (cos, sa, sb))


def kernel(x, c, ctx, c_ctx, w_ada, b_ada, g_norm, w_in, lam_q1, lam_k1, lam_q2, lam_k2,
           g_subln, w_short, w_conf, b_conf, g_conf_ln, b_conf_ln, w_out, g_final):
    b, s, d = x.shape
    n_ctx = ctx.shape[1]
    depth = w_in.shape[0]
    tm_x = min(512, s)
    tm_c = min(256, n_ctx)
    qb_x = min(512, s)
    qb_c = min(256, n_ctx)

    r = -(-(b + 1) // 8) * 8
    cond = jnp.zeros((r, d), F32).at[:b].set(c).at[b].set(c_ctx)
    mods = _modulation(cond, w_ada, b_ada)
    rope_tabs = _rope_tables(s)
    w_in_b = w_in.astype(BF16)
    w_out_b = w_out.astype(BF16)

    for l in range(depth):
        last = l == depth - 1
        lam_init = 0.8 - 0.6 * math.exp(-0.3 * l)
        lam_params = jnp.stack([lam_q1[l], lam_k1[l], lam_q2[l], lam_k2[l]]).astype(F32)
        g_sub = g_subln[l].reshape(V_DIM, 1)
        g_l = g_norm[l].reshape(1, d)
        m = mods[l]
        sh, sc, gt = m[:b, None, 0:d], m[:b, None, d:2 * d], m[:b, None, 2 * d:3 * d]
        bc = lambda v: jnp.broadcast_to(v[None, None, :], (b, 1, d))
        sh_c, sc_c, gt_c = bc(m[b, 0:d]), bc(m[b, d:2 * d]), bc(m[b, 2 * d:3 * d])
        conv_w = (w_short[l], w_conf[l], b_conf[l], g_conf_ln[l], b_conf_ln[l])

        qx, kx, vtx, gatx, ybc = _inproj(x, 1.0 + sc, sh, g_l, w_in_b[l], rope_tabs, conv_w,
                                         kv_only=False, tm=tm_x)
        if not last:
            qc, kc, vtc, gatc, ybc_c = _inproj(ctx, 1.0 + sc_c, sh_c, g_l, w_in_b[l], None, conv_w,
                                               kv_only=False, tm=tm_c)
        else:
            kc, vtc = _inproj(ctx, 1.0 + sc_c, sh_c, g_l, w_in_b[l][:, K_OFF:GA_OFF], None, None,
                              kv_only=True, tm=tm_c)

        ot = _attention(qx, lam_params, (kc, kx), (vtc, vtx), lam_init=lam_init, qb=qb_x)
        x_new = _outproj(ot, gatx, g_sub, ybc, w_out_b[l], x, gt, g_final if last else None,
                         lam_init=lam_init, tm=tm_x)

        if not last:
            ot_c = _attention(qc, lam_params, (kc,), (vtc,), lam_init=lam_init, qb=qb_c)
            ctx = _outproj(ot_c, gatc, g_sub, ybc_c, w_out_b[l], ctx, gt_c, None, lam_init=lam_init, tm=tm_c)
        x = x_new
    return x
```

```python
import functools
import math

import jax
import jax.numpy as jnp
from jax import lax
from jax.experimental import pallas as pl
from jax.experimental.pallas import tpu as pltpu

F32 = jnp.float32
BF16 = jnp.bfloat16

D_MODEL = 1024
GRID_W = 64
N_HEADS = 4
HEAD_DIM = 64
HEAD_QK = 2 * HEAD_DIM
V_DIM = 128
A_WIDTH = N_HEADS * V_DIM
B_WIDTH = 256
C_WIDTH = 256
SHORT_CONV = 3
CONF_CONV = 31
ROPE_BASE = 10000.0
EPS = 1e-6

Q_OFF = 0
K_OFF = 512
V_OFF = 1024
GA_OFF = 1536
B_OFF = 2048
C_OFF = 3072
D_IN = 3840

LANES = 128
MXU_DEPTH = 256
HEADS_PER_PASS = MXU_DEPTH // V_DIM
CONV_HALO = 16
LOG2E = 1.4426950408889634
VMEM_LIMIT = 56 * 1024 * 1024


def _silu(t):
    return t * jax.nn.sigmoid(t)


def _cparams(n_axes):
    return pltpu.CompilerParams(dimension_semantics=("parallel",) * n_axes,
                                vmem_limit_bytes=VMEM_LIMIT)


def _mod_kernel(cond_ref, w_ref, b_ref, o_ref):
    a = _silu(cond_ref[...])
    o_ref[0] = jnp.dot(a.astype(BF16), w_ref[0].astype(BF16),
                       preferred_element_type=F32) + b_ref[0]


def _modulation(cond, w_ada, b_ada):
    depth, d, n = w_ada.shape
    r = cond.shape[0]
    bn = 1024
    return pl.pallas_call(
        _mod_kernel,
        grid=(depth, n // bn),
        in_specs=[pl.BlockSpec((r, d), lambda l, j: (0, 0)),
                  pl.BlockSpec((1, d, bn), lambda l, j: (l, 0, j)),
                  pl.BlockSpec((1, 1, bn), lambda l, j: (l, 0, j))],
        out_specs=pl.BlockSpec((1, r, bn), lambda l, j: (l, 0, j)),
        out_shape=jax.ShapeDtypeStruct((depth, r, n), F32),
        compiler_params=_cparams(2),
        name="modulation",
    )(cond, w_ada, b_ada.reshape(depth, 1, n))


def _taps(win, t, taps):
    rows = win.shape[0]
    acc = None
    for b in range(8):
        group = [(off, w) for off, w in taps if off % 8 == b]
        if not group:
            continue
        wb = win if b == 0 else pltpu.roll(win, rows - b, 0)
        for off, w in group:
            term = wb[off - b:off - b + t] * w
            acc = term if acc is None else acc + term
    return acc


def _rope(t, cos, sa, sb):
    return t * cos + pltpu.roll(t, LANES - 1, 1) * sa + pltpu.roll(t, 1, 1) * sb


def _store_heads_t(dst_ref, v):
    for h in range(N_HEADS):
        dst_ref[0, h] = v[:, h * V_DIM:(h + 1) * V_DIM].T.astype(BF16)


def _inproj_kernel(*refs, rope, kv_only, q_scale, conv_rows):
    x_ref, sc_ref, sh_ref, g_ref, w_ref = refs[:5]
    refs = refs[5:]
    if rope:
        cos_ref, sa_ref, sb_ref = refs[:3]
        refs = refs[3:]
        cos, sa, sb = cos_ref[...], sa_ref[...], sb_ref[...]

    def normed(x):
        ms = jnp.mean(x * x, axis=-1, keepdims=True)
        h = x * lax.rsqrt(ms + EPS) * g_ref[...]
        return (h * sc_ref[0] + sh_ref[0]).astype(BF16)

    hb = normed(x_ref[0])

    def proj(lo, hi):
        return jnp.dot(hb, w_ref[:, lo:hi], preferred_element_type=F32)

    def roped(t, h):
        th = t[:, h * HEAD_QK:(h + 1) * HEAD_QK]
        return _rope(th, cos, sa, sb) if rope else th

    def store_k(dst_ref, t):
        for h in range(N_HEADS):
            dst_ref[0, :, h * HEAD_QK:(h + 1) * HEAD_QK] = roped(t, h).astype(BF16)

    def store_qt(dst_ref, t, scale):
        for h in range(N_HEADS):
            dst_ref[0, h] = (roped(t, h) * scale).T.astype(BF16)

    if kv_only:
        k_ref, vt_ref = refs
        store_k(k_ref, proj(0, 512))
        _store_heads_t(vt_ref, proj(512, 1024))
        return

    xp_ref, xn_ref, wsh_ref, wcf_ref, bcf_ref, gln_ref, bln_ref = refs[:7]
    qt_ref, k_ref, vt_ref, gat_ref, ybc_ref = refs[7:]
    tm = hb.shape[0]

    hb_ext = jnp.concatenate([normed(xp_ref[0, 0]), hb, normed(xn_ref[0, 0])], axis=0)
    pb = jnp.dot(hb_ext, w_ref[:, B_OFF:C_OFF], preferred_element_type=F32)
    pc = jnp.dot(hb_ext, w_ref[:, C_OFF:D_IN], preferred_element_type=F32)
    j = pl.program_id(1)
    row = lax.broadcasted_iota(jnp.int32, (tm + 2 * CONV_HALO, 1), 0)
    inside = jnp.logical_and(jnp.logical_or(row >= CONV_HALO, j > 0),
                             jnp.logical_or(row < tm + CONV_HALO, j < pl.num_programs(1) - 1))
    keep = jnp.where(inside, 1.0, 0.0)
    ub = pb[:, 256:512] * pb[:, 512:768] * keep
    uc = pc[:, 0:256] * jax.nn.sigmoid(pc[:, 256:512]) * keep
    body = slice(CONV_HALO, CONV_HALO + tm)
    gate_b = pb[body, 0:256] * _silu(pb[body, 768:1024])
    gate_c = _silu(pc[body, 512:768])
    others = [lambda: store_qt(qt_ref, proj(Q_OFF, K_OFF), q_scale),
              lambda: store_k(k_ref, proj(K_OFF, V_OFF)),
              lambda: _store_heads_t(vt_ref, proj(V_OFF, GA_OFF)),
              lambda: _store_heads_t(gat_ref, proj(GA_OFF, B_OFF))]
    n_blocks = tm // conv_rows
    per_block = -(-len(others) // n_blocks)
    for r0 in range(0, tm, conv_rows):
        rows = slice(r0, r0 + conv_rows)
        half = SHORT_CONV // 2
        conv_b = _taps(ub[r0 + 8:r0 + conv_rows + 24], conv_rows,
                       [(8 - half + k, wsh_ref[k:k + 1, :]) for k in range(SHORT_CONV)])
        half = CONF_CONV // 2
        conv_c = _taps(uc[r0:r0 + conv_rows + 2 * CONV_HALO], conv_rows,
                       [(CONV_HALO - half + k, wcf_ref[k:k + 1, :]) for k in range(CONF_CONV)])
        u = conv_c + bcf_ref[...]
        mu = jnp.mean(u, axis=-1, keepdims=True)
        uz = u - mu
        var = jnp.mean(uz * uz, axis=-1, keepdims=True)
        ln = uz * lax.rsqrt(var + EPS) * gln_ref[...] + bln_ref[...]
        ybc_ref[0, rows, 0:256] = (gate_b[rows] * conv_b).astype(BF16)
        ybc_ref[0, rows, 256:512] = (_silu(ln) * gate_c[rows]).astype(BF16)
        for emit in others[:per_block]:
            emit()
        others = others[per_block:]


def _inproj(x, sc1p, sh, g, w, rope_tabs, conv_w, *, kv_only, tm):
    b, s, d = x.shape
    rope = rope_tabs is not None
    n = w.shape[1]
    grid = (b, s // tm)
    in_specs = [pl.BlockSpec((1, tm, d), lambda i, j: (i, j, 0)),
                pl.BlockSpec((1, 1, d), lambda i, j: (i, 0, 0)),
                pl.BlockSpec((1, 1, d), lambda i, j: (i, 0, 0)),
                pl.BlockSpec((1, d), lambda i, j: (0, 0)),
                pl.BlockSpec((d, n), lambda i, j: (0, 0))]
    args = [x, sc1p, sh, g, w]
    if rope:
        in_specs += [pl.BlockSpec((tm, LANES), lambda i, j: (j, 0))] * 3
        args += list(rope_tabs)
    if not kv_only:
        w_short, w_conf, b_conf, g_ln, b_ln = conv_w
        hb_per_tile = tm // CONV_HALO
        n_hb = s // CONV_HALO
        halo = lambda index: pl.BlockSpec((1, 1, CONV_HALO, d), index)
        vec = lambda rows: pl.BlockSpec((rows, 256), lambda i, j: (0, 0))
        x_blocks = x.reshape(b, n_hb, CONV_HALO, d)
        in_specs += [halo(lambda i, j: (i, jnp.maximum(j * hb_per_tile - 1, 0), 0, 0)),
                     halo(lambda i, j: (i, jnp.minimum((j + 1) * hb_per_tile, n_hb - 1), 0, 0)),
                     vec(SHORT_CONV), vec(CONF_CONV), vec(1), vec(1), vec(1)]
        args += [x_blocks, x_blocks, w_short, w_conf, b_conf.reshape(1, -1), g_ln.reshape(1, -1),
                 b_ln.reshape(1, -1)]
    tok = lambda width: (jax.ShapeDtypeStruct((b, s, width), BF16),
                         pl.BlockSpec((1, tm, width), lambda i, j: (i, j, 0)))
    heads_t = (jax.ShapeDtypeStruct((b, N_HEADS, V_DIM, s), BF16),
               pl.BlockSpec((1, N_HEADS, V_DIM, tm), lambda i, j: (i, 0, 0, j)))
    outs = [tok(512), heads_t] if kv_only else [heads_t, tok(512), heads_t, heads_t, tok(512)]
    q_scale = LOG2E * HEAD_DIM ** -0.5
    return pl.pallas_call(
        functools.partial(_inproj_kernel, rope=rope, kv_only=kv_only, q_scale=q_scale,
                          conv_rows=min(256, tm)),
        grid=grid,
        in_specs=in_specs,
        out_specs=[o[1] for o in outs],
        out_shape=[o[0] for o in outs],
        compiler_params=_cparams(2),
        name="inproj",
    )(*args)


def _attn_kernel(*refs, seg_lens, chunk, lam_init):
    n_seg = len(seg_lens)
    q_ref, lamp_ref = refs[:2]
    k_refs = refs[2:2 + n_seg]
    vt_refs = refs[2 + n_seg:2 + 2 * n_seg]
    o_ref = refs[2 + 2 * n_seg]
    even, odd = refs[3 + 2 * n_seg:7 + 2 * n_seg], refs[7 + 2 * n_seg:]
    t = pl.program_id(0)

    @pl.when(t == 0)
    def _():
        s_odd, mx_odd, acc_odd, den_odd = odd
        s_odd[...] = jnp.zeros(s_odd.shape, F32)
        mx_odd[...] = jnp.zeros(mx_odd.shape, F32)
        acc_odd[...] = jnp.zeros(acc_odd.shape, F32)
        den_odd[...] = jnp.ones(den_odd.shape, F32)

    chunks = []
    off = 0
    for si, n in enumerate(seg_lens):
        for c0 in range(0, n, chunk):
            chunks.append((si, c0, min(chunk, n - c0), off + c0))
        off += n

    def step(wr, rd):
        s_wr, mx_wr, acc_wr, den_wr = wr
        s_rd, mx_rd, acc_rd, den_rd = rd

        lp = lamp_ref[...]
        lam = (jnp.exp(jnp.sum(lp[0:1] * lp[1:2], axis=-1, keepdims=True))
               - jnp.exp(jnp.sum(lp[2:3] * lp[3:4], axis=-1, keepdims=True)) + lam_init)
        r0 = 1.0 / den_rd[0][0:1]
        r1 = lam / den_rd[1][0:1]
        o_ref[0, 0] = (acc_rd[0] * r0 - acc_rd[1] * r1).astype(BF16)

        qt = q_ref[0, 0]
        dim = lax.broadcasted_iota(jnp.int32, qt.shape, 0)
        zero = jnp.zeros_like(qt)
        q_maps = (jnp.where(dim < HEAD_DIM, qt, zero), jnp.where(dim >= HEAD_DIM, qt, zero))
        for m in range(2):
            mx_new = None
            mx_old = mx_rd[m][0:1]
            a = None
            den = None
            for si, c0, cn, r0 in chunks:
                s = jnp.dot(k_refs[si][0, c0:c0 + cn, :], q_maps[m], preferred_element_type=F32)
                s_wr[m, r0:r0 + cn, :] = s
                cm = jnp.max(s, axis=0, keepdims=True)
                mx_new = cm if mx_new is None else jnp.maximum(mx_new, cm)

                e = jnp.exp2(s_rd[m, r0:r0 + cn, :] - mx_old)
                ds = jnp.sum(e, axis=0, keepdims=True)
                den = ds if den is None else den + ds
                pv = jnp.dot(vt_refs[si][0, 0, :, c0:c0 + cn], e.astype(BF16), preferred_element_type=F32)
                a = pv if a is None else a + pv
            mx_wr[m] = jnp.broadcast_to(mx_new, mx_wr.shape[1:])
            den_wr[m] = jnp.broadcast_to(den, den_wr.shape[1:])
            acc_wr[m] = a

    @pl.when(t % 2 == 0)
    def _():
        step(even, odd)

    @pl.when(t % 2 == 1)
    def _():
        step(odd, even)


def _attention(q, lam_params, ks, vts, *, lam_init, qb):
    b, _, _, s = q.shape
    seg_lens = tuple(k.shape[1] for k in ks)
    ktot = sum(seg_lens)
    nq = s // qb
    n_units = b * N_HEADS * nq

    def unit(t, lag):
        u = jnp.clip(t - lag, 0, n_units - 1)
        return u // (N_HEADS * nq), (u // nq) % N_HEADS, u % nq

    def q_a(t):
        i, h, j = unit(t, 0)
        return i, h, 0, j

    def out_c(t):
        i, h, j = unit(t, 2)
        return i, h, 0, j

    def keys_a(t):
        i, h, _ = unit(t, 0)
        return i, 0, h

    def vals_b(t):
        i, h, _ = unit(t, 1)
        return i, h, 0, 0

    in_specs = [pl.BlockSpec((1, 1, HEAD_QK, qb), q_a),
                pl.BlockSpec((4, HEAD_DIM), lambda t: (0, 0))]
    in_specs += [pl.BlockSpec((1, n, HEAD_QK), keys_a) for n in seg_lens]
    in_specs += [pl.BlockSpec((1, 1, V_DIM, n), vals_b) for n in seg_lens]
    parity_scratch = [pltpu.VMEM((2, ktot, qb), F32), pltpu.VMEM((2, 8, qb), F32),
                      pltpu.VMEM((2, V_DIM, qb), F32), pltpu.VMEM((2, 8, qb), F32)]
    return pl.pallas_call(
        functools.partial(_attn_kernel, seg_lens=seg_lens, chunk=512, lam_init=lam_init),
        grid=(n_units + 2,),
        in_specs=in_specs,
        out_specs=pl.BlockSpec((1, 1, V_DIM, qb), out_c),
        out_shape=jax.ShapeDtypeStruct((b, N_HEADS, V_DIM, s), BF16),
        scratch_shapes=parity_scratch * 2,
        compiler_params=pltpu.CompilerParams(dimension_semantics=("arbitrary",),
                                             vmem_limit_bytes=VMEM_LIMIT),
        name="attention",
    )(q, lam_params, *ks, *vts)


def _outproj_kernel(ot_ref, gat_ref, gsub_ref, ybc_ref, w_ref, x_ref, gt_ref, *rest, final, lam_init):
    y = jnp.dot(ybc_ref[0], w_ref[A_WIDTH:, :], preferred_element_type=F32)
    g = gsub_ref[...] * (1.0 - lam_init)
    ya = []
    for h in range(N_HEADS):
        o = ot_ref[0, h].astype(F32)
        ms = jnp.mean(o * o, axis=0, keepdims=True)
        ya.append((o * lax.rsqrt(ms + EPS) * g * _silu(gat_ref[0, h].astype(F32))).astype(BF16))
    for h in range(0, N_HEADS, HEADS_PER_PASS):
        lhs = jnp.concatenate(ya[h:h + HEADS_PER_PASS], axis=0)
        y = y + lax.dot_general(lhs, w_ref[h * V_DIM:(h + HEADS_PER_PASS) * V_DIM, :],
                                (((0,), (0,)), ((), ())), preferred_element_type=F32)
    xn = x_ref[0] + gt_ref[0] * y
    if final:
        gf_ref, o_ref = rest
        ms = jnp.mean(xn * xn, axis=-1, keepdims=True)
        o_ref[0] = xn * lax.rsqrt(ms + EPS) * gf_ref[...]
    else:
        (o_ref,) = rest
        o_ref[0] = xn


def _outproj(ot, gat, g_sub, ybc, w, x, gt, g_final, *, lam_init, tm):
    b, s, d = x.shape
    tok = lambda width: pl.BlockSpec((1, tm, width), lambda i, j: (i, j, 0))
    heads_t = pl.BlockSpec((1, N_HEADS, V_DIM, tm), lambda i, j: (i, 0, 0, j))
    in_specs = [heads_t, heads_t, pl.BlockSpec((V_DIM, 1), lambda i, j: (0, 0)), tok(512),
                pl.BlockSpec((d, d), lambda i, j: (0, 0)), tok(d),
                pl.BlockSpec((1, 1, d), lambda i, j: (i, 0, 0))]
    args = [ot, gat, g_sub, ybc, w, x, gt]
    final = g_final is not None
    if final:
        in_specs.append(pl.BlockSpec((1, d), lambda i, j: (0, 0)))
        args.append(g_final.reshape(1, d))
    return pl.pallas_call(
        functools.partial(_outproj_kernel, final=final, lam_init=lam_init),
        grid=(b, s // tm),
        in_specs=in_specs,
        out_specs=tok(d),
        out_shape=jax.ShapeDtypeStruct((b, s, d), F32),
        compiler_params=_cparams(2),
        name="outproj",
    )(*args)


def _rope_tables(s):
    rows = s // GRID_W
    r = jnp.repeat(jnp.arange(rows, dtype=F32), GRID_W)
    col = jnp.tile(jnp.arange(GRID_W, dtype=F32), rows)
    half = HEAD_DIM // 2
    inv_freq = ROPE_BASE ** (-jnp.arange(0, half, 2, dtype=F32) / half)
    ang = jnp.concatenate([r[:, None] * inv_freq, col[:, None] * inv_freq], axis=-1)
    cos = jnp.repeat(jnp.cos(ang), 2, axis=-1)
    sin = jnp.repeat(jnp.sin(ang), 2, axis=-1)
    even = (jnp.arange(HEAD_DIM) % 2 == 0)[None, :]
    sa = jnp.where(even, -sin, 0.0)
    sb = jnp.where(even, 0.0, sin)
    return tuple(jnp.tile(t, (1, LANES // HEAD_DIM)) for t in (cos, sa, sb))


def kernel(x, c, ctx, c_ctx, w_ada, b_ada, g_norm, w_in, lam_q1, lam_k1, lam_q2, lam_k2,
           g_subln, w_short, w_conf, b_conf, g_conf_ln, b_conf_ln, w_out, g_final):
    b, s, d = x.shape
    n_ctx = ctx.shape[1]
    depth = w_in.shape[0]
    tm_x = min(512, s)
    tm_c = min(256, n_ctx)
    qb_x = min(512, s)
    qb_c = min(256, n_ctx)

    r = -(-(b + 1) // 8) * 8
    cond = jnp.zeros((r, d), F32).at[:b].set(c).at[b].set(c_ctx)
    mods = _modulation(cond, w_ada, b_ada)
    rope_tabs = _rope_tables(s)
    w_in_b = w_in.astype(BF16)
    w_out_b = w_out.astype(BF16)

    for l in range(depth):
        last = l == depth - 1
        lam_init = 0.8 - 0.6 * math.exp(-0.3 * l)
        lam_params = jnp.stack([lam_q1[l], lam_k1[l], lam_q2[l], lam_k2[l]]).astype(F32)
        g_sub = g_subln[l].reshape(V_DIM, 1)
        g_l = g_norm[l].reshape(1, d)
        m = mods[l]
        sh, sc, gt = m[:b, None, 0:d], m[:b, None, d:2 * d], m[:b, None, 2 * d:3 * d]
        bc = lambda v: jnp.broadcast_to(v[None, None, :], (b, 1, d))
        sh_c, sc_c, gt_c = bc(m[b, 0:d]), bc(m[b, d:2 * d]), bc(m[b, 2 * d:3 * d])
        conv_w = (w_short[l], w_conf[l], b_conf[l], g_conf_ln[l], b_conf_ln[l])

        qx, kx, vtx, gatx, ybc = _inproj(x, 1.0 + sc, sh, g_l, w_in_b[l], rope_tabs, conv_w,
                                         kv_only=False, tm=tm_x)
        if not last:
            qc, kc, vtc, gatc, ybc_c = _inproj(ctx, 1.0 + sc_c, sh_c, g_l, w_in_b[l], None, conv_w,
                                               kv_only=False, tm=tm_c)
        else:
            kc, vtc = _inproj(ctx, 1.0 + sc_c, sh_c, g_l, w_in_b[l][:, K_OFF:GA_OFF], None, None,
                              kv_only=True, tm=tm_c)

        ot = _attention(qx, lam_params, (kc, kx), (vtc, vtx), lam_init=lam_init, qb=qb_x)
        x_new = _outproj(ot, gatx, g_sub, ybc, w_out_b[l], x, gt, g_final if last else None,
                         lam_init=lam_init, tm=tm_x)

        if not last:
            ot_c = _attention(qc, lam_params, (kc,), (vtc,), lam_init=lam_init, qb=qb_c)
            ctx = _outproj(ot_c, gatc, g_sub, ybc_c, w_out_b[l], ctx, gt_c, None, lam_init=lam_init, tm=tm_c)
        x = x_new
    return x
```

```python
import functools
import math

import jax
import jax.numpy as jnp
from jax import lax
from jax.experimental import pallas as pl
from jax.experimental.pallas import tpu as pltpu

F32 = jnp.float32
BF16 = jnp.bfloat16

D_MODEL = 1024
GRID_W = 64
N_HEADS = 4
HEAD_DIM = 64
HEAD_QK = 2 * HEAD_DIM
V_DIM = 128
A_WIDTH = N_HEADS * V_DIM
B_WIDTH = 256
C_WIDTH = 256
SHORT_CONV = 3
CONF_CONV = 31
ROPE_BASE = 10000.0
EPS = 1e-6

Q_OFF = 0
K_OFF = 512
V_OFF = 1024
GA_OFF = 1536
B_OFF = 2048
C_OFF = 3072
D_IN = 3840

LANES = 128
MXU_DEPTH = 256
HEADS_PER_PASS = MXU_DEPTH // V_DIM
CONV_HALO = 16
LOG2E = 1.4426950408889634
VMEM_LIMIT = 56 * 1024 * 1024


def _silu(t):
    return t * jax.nn.sigmoid(t)


def _cparams(n_axes):
    return pltpu.CompilerParams(dimension_semantics=("parallel",) * n_axes,
                                vmem_limit_bytes=VMEM_LIMIT)


def _mod_kernel(cond_ref, w_ref, b_ref, o_ref):
    a = _silu(cond_ref[...])
    o_ref[0] = jnp.dot(a.astype(BF16), w_ref[0].astype(BF16),
                       preferred_element_type=F32) + b_ref[0]


def _modulation(cond, w_ada, b_ada):
    depth, d, n = w_ada.shape
    r = cond.shape[0]
    bn = 1024
    return pl.pallas_call(
        _mod_kernel,
        grid=(depth, n // bn),
        in_specs=[pl.BlockSpec((r, d), lambda l, j: (0, 0)),
                  pl.BlockSpec((1, d, bn), lambda l, j: (l, 0, j)),
                  pl.BlockSpec((1, 1, bn), lambda l, j: (l, 0, j))],
        out_specs=pl.BlockSpec((1, r, bn), lambda l, j: (l, 0, j)),
        out_shape=jax.ShapeDtypeStruct((depth, r, n), F32),
        compiler_params=_cparams(2),
        name="modulation",
    )(cond, w_ada, b_ada.reshape(depth, 1, n))


def _taps(win, t, taps):
    rows = win.shape[0]
    acc = None
    for b in range(8):
        group = [(off, w) for off, w in taps if off % 8 == b]
        if not group:
            continue
        wb = win if b == 0 else pltpu.roll(win, rows - b, 0)
        for off, w in group:
            term = wb[off - b:off - b + t] * w
            acc = term if acc is None else acc + term
    return acc


def _rope(t, cos, sa, sb):
    return t * cos + pltpu.roll(t, LANES - 1, 1) * sa + pltpu.roll(t, 1, 1) * sb


def _store_heads_t(dst_ref, v):
    for h in range(N_HEADS):
        dst_ref[0, h] = v[:, h * V_DIM:(h + 1) * V_DIM].T.astype(BF16)


def _inproj_kernel(*refs, rope, kv_only, q_scale, conv_rows):
    x_ref, sc_ref, sh_ref, g_ref, w_ref = refs[:5]
    refs = refs[5:]
    if rope:
        cos_ref, sa_ref, sb_ref = refs[:3]
        refs = refs[3:]
        cos, sa, sb = cos_ref[...], sa_ref[...], sb_ref[...]

    def normed(x):
        ms = jnp.mean(x * x, axis=-1, keepdims=True)
        h = x * lax.rsqrt(ms + EPS) * g_ref[...]
        return (h * sc_ref[0] + sh_ref[0]).astype(BF16)

    hb = normed(x_ref[0])

    def proj(lo, hi):
        return jnp.dot(hb, w_ref[:, lo:hi], preferred_element_type=F32)

    def roped(t, h):
        th = t[:, h * HEAD_QK:(h + 1) * HEAD_QK]
        return _rope(th, cos, sa, sb) if rope else th

    def store_k(dst_ref, t):
        for h in range(N_HEADS):
            dst_ref[0, :, h * HEAD_QK:(h + 1) * HEAD_QK] = roped(t, h).astype(BF16)

    def store_qt(dst_ref, t, scale):
        for h in range(N_HEADS):
            dst_ref[0, h] = (roped(t, h) * scale).T.astype(BF16)

    if kv_only:
        k_ref, vt_ref = refs
        store_k(k_ref, proj(0, 512))
        _store_heads_t(vt_ref, proj(512, 1024))
        return

    xp_ref, xn_ref, wsh_ref, wcf_ref, bcf_ref, gln_ref, bln_ref = refs[:7]
    qt_ref, k_ref, vt_ref, gat_ref, ybc_ref = refs[7:]
    tm = hb.shape[0]

    hb_ext = jnp.concatenate([normed(xp_ref[0, 0]), hb, normed(xn_ref[0, 0])], axis=0)
    pb = jnp.dot(hb_ext, w_ref[:, B_OFF:C_OFF], preferred_element_type=F32)
    pc = jnp.dot(hb_ext, w_ref[:, C_OFF:D_IN], preferred_element_type=F32)
    j = pl.program_id(1)
    row = lax.broadcasted_iota(jnp.int32, (tm + 2 * CONV_HALO, 1), 0)
    inside = jnp.logical_and(jnp.logical_or(row >= CONV_HALO, j > 0),
                             jnp.logical_or(row < tm + CONV_HALO, j < pl.num_programs(1) - 1))
    keep = jnp.where(inside, 1.0, 0.0)
    ub = pb[:, 256:512] * pb[:, 512:768] * keep
    uc = pc[:, 0:256] * jax.nn.sigmoid(pc[:, 256:512]) * keep
    body = slice(CONV_HALO, CONV_HALO + tm)
    gate_b = pb[body, 0:256] * _silu(pb[body, 768:1024])
    gate_c = _silu(pc[body, 512:768])
    others = [lambda: store_qt(qt_ref, proj(Q_OFF, K_OFF), q_scale),
              lambda: store_k(k_ref, proj(K_OFF, V_OFF)),
              lambda: _store_heads_t(vt_ref, proj(V_OFF, GA_OFF)),
              lambda: _store_heads_t(gat_ref, proj(GA_OFF, B_OFF))]
    n_blocks = tm // conv_rows
    per_block = -(-len(others) // n_blocks)
    for r0 in range(0, tm, conv_rows):
        rows = slice(r0, r0 + conv_rows)
        half = SHORT_CONV // 2
        conv_b = _taps(ub[r0 + 8:r0 + conv_rows + 24], conv_rows,
                       [(8 - half + k, wsh_ref[k:k + 1, :]) for k in range(SHORT_CONV)])
        half = CONF_CONV // 2
        conv_c = _taps(uc[r0:r0 + conv_rows + 2 * CONV_HALO], conv_rows,
                       [(CONV_HALO - half + k, wcf_ref[k:k + 1, :]) for k in range(CONF_CONV)])
        u = conv_c + bcf_ref[...]
        mu = jnp.mean(u, axis=-1, keepdims=True)
        uz = u - mu
        var = jnp.mean(uz * uz, axis=-1, keepdims=True)
        ln = uz * lax.rsqrt(var + EPS) * gln_ref[...] + bln_ref[...]
        ybc_ref[0, rows, 0:256] = (gate_b[rows] * conv_b).astype(BF16)
        ybc_ref[0, rows, 256:512] = (_silu(ln) * gate_c[rows]).astype(BF16)
        for emit in others[:per_block]:
            emit()
        others = others[per_block:]


def _inproj(x, sc1p, sh, g, w, rope_tabs, conv_w, *, kv_only, tm):
    b, s, d = x.shape
    rope = rope_tabs is not None
    n = w.shape[1]
    grid = (b, s // tm)
    in_specs = [pl.BlockSpec((1, tm, d), lambda i, j: (i, j, 0)),
                pl.BlockSpec((1, 1, d), lambda i, j: (i, 0, 0)),
                pl.BlockSpec((1, 1, d), lambda i, j: (i, 0, 0)),
                pl.BlockSpec((1, d), lambda i, j: (0, 0)),
                pl.BlockSpec((d, n), lambda i, j: (0, 0))]
    args = [x, sc1p, sh, g, w]
    if rope:
        in_specs += [pl.BlockSpec((tm, LANES), lambda i, j: (j, 0))] * 3
        args += list(rope_tabs)
    if not kv_only:
        w_short, w_conf, b_conf, g_ln, b_ln = conv_w
        hb_per_tile = tm // CONV_HALO
        n_hb = s // CONV_HALO
        halo = lambda index: pl.BlockSpec((1, 1, CONV_HALO, d), index)
        vec = lambda rows: pl.BlockSpec((rows, 256), lambda i, j: (0, 0))
        x_blocks = x.reshape(b, n_hb, CONV_HALO, d)
        in_specs += [halo(lambda i, j: (i, jnp.maximum(j * hb_per_tile - 1, 0), 0, 0)),
                     halo(lambda i, j: (i, jnp.minimum((j + 1) * hb_per_tile, n_hb - 1), 0, 0)),
                     vec(SHORT_CONV), vec(CONF_CONV), vec(1), vec(1), vec(1)]
        args += [x_blocks, x_blocks, w_short, w_conf, b_conf.reshape(1, -1), g_ln.reshape(1, -1),
                 b_ln.reshape(1, -1)]
    tok = lambda width: (jax.ShapeDtypeStruct((b, s, width), BF16),
                         pl.BlockSpec((1, tm, width), lambda i, j: (i, j, 0)))
    heads_t = (jax.ShapeDtypeStruct((b, N_HEADS, V_DIM, s), BF16),
               pl.BlockSpec((1, N_HEADS, V_DIM, tm), lambda i, j: (i, 0, 0, j)))
    outs = [tok(512), heads_t] if kv_only else [heads_t, tok(512), heads_t, heads_t, tok(512)]
    q_scale = LOG2E * HEAD_DIM ** -0.5
    return pl.pallas_call(
        functools.partial(_inproj_kernel, rope=rope, kv_only=kv_only, q_scale=q_scale,
                          conv_rows=min(256, tm)),
        grid=grid,
        in_specs=in_specs,
        out_specs=[o[1] for o in outs],
        out_shape=[o[0] for o in outs],
        compiler_params=_cparams(2),
        name="inproj",
    )(*args)


def _attn_kernel(*refs, seg_lens, chunk, lam_init):
    n_seg = len(seg_lens)
    q_ref, lamp_ref = refs[:2]
    k_refs = refs[2:2 + n_seg]
    vt_refs = refs[2 + n_seg:2 + 2 * n_seg]
    o_ref = refs[2 + 2 * n_seg]
    even, odd = refs[3 + 2 * n_seg:7 + 2 * n_seg], refs[7 + 2 * n_seg:]
    t = pl.program_id(0)

    @pl.when(t == 0)
    def _():
        s_odd, mx_odd, acc_odd, den_odd = odd
        s_odd[...] = jnp.zeros(s_odd.shape, F32)
        mx_odd[...] = jnp.zeros(mx_odd.shape, F32)
        acc_odd[...] = jnp.zeros(acc_odd.shape, F32)
        den_odd[...] = jnp.ones(den_odd.shape, F32)

    chunks = []
    off = 0
    for si, n in enumerate(seg_lens):
        for c0 in range(0, n, chunk):
            chunks.append((si, c0, min(chunk, n - c0), off + c0))
        off += n

    def step(wr, rd):
        s_wr, mx_wr, acc_wr, den_wr = wr
        s_rd, mx_rd, acc_rd, den_rd = rd

        lp = lamp_ref[...]
        lam = (jnp.exp(jnp.sum(lp[0:1] * lp[1:2], axis=-1, keepdims=True))
               - jnp.exp(jnp.sum(lp[2:3] * lp[3:4], axis=-1, keepdims=True)) + lam_init)
        r0 = 1.0 / den_rd[0][0:1]
        r1 = lam / den_rd[1][0:1]
        o_ref[0, 0] = (acc_rd[0] * r0 - acc_rd[1] * r1).astype(BF16)

        qt = q_ref[0, 0]
        dim = lax.broadcasted_iota(jnp.int32, qt.shape, 0)
        zero = jnp.zeros_like(qt)
        q_maps = (jnp.where(dim < HEAD_DIM, qt, zero), jnp.where(dim >= HEAD_DIM, qt, zero))
        for m in range(2):
            mx_new = None
            mx_old = mx_rd[m][0:1]
            a = None
            den = None
            for si, c0, cn, r0 in chunks:
                s = jnp.dot(k_refs[si][0, c0:c0 + cn, :], q_maps[m], preferred_element_type=F32)
                s_wr[m, r0:r0 + cn, :] = s
                cm = jnp.max(s, axis=0, keepdims=True)
                mx_new = cm if mx_new is None else jnp.maximum(mx_new, cm)

                e = jnp.exp2(s_rd[m, r0:r0 + cn, :] - mx_old)
                ds = jnp.sum(e, axis=0, keepdims=True)
                den = ds if den is None else den + ds
                pv = jnp.dot(vt_refs[si][0, 0, :, c0:c0 + cn], e.astype(BF16), preferred_element_type=F32)
                a = pv if a is None else a + pv
            mx_wr[m] = jnp.broadcast_to(mx_new, mx_wr.shape[1:])
            den_wr[m] = jnp.broadcast_to(den, den_wr.shape[1:])
            acc_wr[m] = a

    @pl.when(t % 2 == 0)
    def _():
        step(even, odd)

    @pl.when(t % 2 == 1)
    def _():
        step(odd, even)


def _attention(q, lam_params, ks, vts, *, lam_init, qb):
    b, _, _, s = q.shape
    seg_lens = tuple(k.shape[1] for k in ks)
    ktot = sum(seg_lens)
    nq = s // qb
    n_units = b * N_HEADS * nq

    def unit(t, lag):
        u = jnp.clip(t - lag, 0, n_units - 1)
        return u // (N_HEADS * nq), (u // nq) % N_HEADS, u % nq

    def q_a(t):
        i, h, j = unit(t, 0)
        return i, h, 0, j

    def out_c(t):
        i, h, j = unit(t, 2)
        return i, h, 0, j

    def keys_a(t):
        i, h, _ = unit(t, 0)
        return i, 0, h

    def vals_b(t):
        i, h, _ = unit(t, 1)
        return i, h, 0, 0

    in_specs = [pl.BlockSpec((1, 1, HEAD_QK, qb), q_a),
                pl.BlockSpec((4, HEAD_DIM), lambda t: (0, 0))]
    in_specs += [pl.BlockSpec((1, n, HEAD_QK), keys_a) for n in seg_lens]
    in_specs += [pl.BlockSpec((1, 1, V_DIM, n), vals_b) for n in seg_lens]
    parity_scratch = [pltpu.VMEM((2, ktot, qb), F32), pltpu.VMEM((2, 8, qb), F32),
                      pltpu.VMEM((2, V_DIM, qb), F32), pltpu.VMEM((2, 8, qb), F32)]
    return pl.pallas_call(
        functools.partial(_attn_kernel, seg_lens=seg_lens, chunk=512, lam_init=lam_init),
        grid=(n_units + 2,),
        in_specs=in_specs,
        out_specs=pl.BlockSpec((1, 1, V_DIM, qb), out_c),
        out_shape=jax.ShapeDtypeStruct((b, N_HEADS, V_DIM, s), BF16),
        scratch_shapes=parity_scratch * 2,
        compiler_params=pltpu.CompilerParams(dimension_semantics=("arbitrary",),
                                             vmem_limit_bytes=VMEM_LIMIT),
        name="attention",
    )(q, lam_params, *ks, *vts)


def _outproj_kernel(ot_ref, gat_ref, gsub_ref, ybc_ref, w_ref, x_ref, gt_ref, *rest, final, lam_init):
    y = jnp.dot(ybc_ref[0], w_ref[A_WIDTH:, :], preferred_element_type=F32)
    g = gsub_ref[...] * (1.0 - lam_init)
    ya = []
    for h in range(N_HEADS):
        o = ot_ref[0, h].astype(F32)
        ms = jnp.mean(o * o, axis=0, keepdims=True)
        ya.append((o * lax.rsqrt(ms + EPS) * g * _silu(gat_ref[0, h].astype(F32))).astype(BF16))
    for h in range(0, N_HEADS, HEADS_PER_PASS):
        lhs = jnp.concatenate(ya[h:h + HEADS_PER_PASS], axis=0)
        y = y + lax.dot_general(lhs, w_ref[h * V_DIM:(h + HEADS_PER_PASS) * V_DIM, :],
                                (((0,), (0,)), ((), ())), preferred_element_type=F32)
    xn = x_ref[0] + gt_ref[0] * y
    if final:
        gf_ref, o_ref = rest
        ms = jnp.mean(xn * xn, axis=-1, keepdims=True)
        o_ref[0] = xn * lax.rsqrt(ms + EPS) * gf_ref[...]
    else:
        (o_ref,) = rest
        o_ref[0] = xn


def _outproj(ot, gat, g_sub, ybc, w, x, gt, g_final, *, lam_init, tm):
    b, s, d = x.shape
    tok = lambda width: pl.BlockSpec((1, tm, width), lambda i, j: (i, j, 0))
    heads_t = pl.BlockSpec((1, N_HEADS, V_DIM, tm), lambda i, j: (i, 0, 0, j))
    in_specs = [heads_t, heads_t, pl.BlockSpec((V_DIM, 1), lambda i, j: (0, 0)), tok(512),
                pl.BlockSpec((d, d), lambda i, j: (0, 0)), tok(d),
                pl.BlockSpec((1, 1, d), lambda i, j: (i, 0, 0))]
    args = [ot, gat, g_sub, ybc, w, x, gt]
    final = g_final is not None
    if final:
        in_specs.append(pl.BlockSpec((1, d), lambda i, j: (0, 0)))
        args.append(g_final.reshape(1, d))
    return pl.pallas_call(
        functools.partial(_outproj_kernel, final=final, lam_init=lam_init),
        grid=(b, s // tm),
        in_specs=in_specs,
        out_specs=tok(d),
        out_shape=jax.ShapeDtypeStruct((b, s, d), F32),
        compiler_params=_cparams(2),
        name="outproj",
    )(*args)


def _rope_tables(s):
    rows = s // GRID_W
    r = jnp.repeat(jnp.arange(rows, dtype=F32), GRID_W)
    col = jnp.tile(jnp.arange(GRID_W, dtype=F32), rows)
    half = HEAD_DIM // 2
    inv_freq = ROPE_BASE ** (-jnp.arange(0, half, 2, dtype=F32) / half)
    ang = jnp.concatenate([r[:, None] * inv_freq, col[:, None] * inv_freq], axis=-1)
    cos = jnp.repeat(jnp.cos(ang), 2, axis=-1)
    sin = jnp.repeat(jnp.sin(ang), 2, axis=-1)
    even = (jnp.arange(HEAD_DIM) % 2 == 0)[None, :]
    sa = jnp.where(even, -sin, 0.0)
    sb = jnp.where(even, 0.0, sin)
    return tuple(jnp.tile(t, (1, LANES // HEAD_DIM)) for t in (cos, sa, sb))


def kernel(x, c, ctx, c_ctx, w_ada, b_ada, g_norm, w_in, lam_q1, lam_k1, lam_q2, lam_k2,
           g_subln, w_short, w_conf, b_conf, g_conf_ln, b_conf_ln, w_out, g_final):
    b, s, d = x.shape
    n_ctx = ctx.shape[1]
    depth = w_in.shape[0]
    tm_x = min(512, s)
    tm_c = min(256, n_ctx)
    qb_x = min(512, s)
    qb_c = min(256, n_ctx)
    tm_out = min(1024, s)

    r = -(-(b + 1) // 8) * 8
    cond = jnp.zeros((r, d), F32).at[:b].set(c).at[b].set(c_ctx)
    mods = _modulation(cond, w_ada, b_ada)
    rope_tabs = _rope_tables(s)
    w_in_b = w_in.astype(BF16)
    w_out_b = w_out.astype(BF16)

    for l in range(depth):
        last = l == depth - 1
        lam_init = 0.8 - 0.6 * math.exp(-0.3 * l)
        lam_params = jnp.stack([lam_q1[l], lam_k1[l], lam_q2[l], lam_k2[l]]).astype(F32)
        g_sub = g_subln[l].reshape(V_DIM, 1)
        g_l = g_norm[l].reshape(1, d)
        m = mods[l]
        sh, sc, gt = m[:b, None, 0:d], m[:b, None, d:2 * d], m[:b, None, 2 * d:3 * d]
        bc = lambda v: jnp.broadcast_to(v[None, None, :], (b, 1, d))
        sh_c, sc_c, gt_c = bc(m[b, 0:d]), bc(m[b, d:2 * d]), bc(m[b, 2 * d:3 * d])
        conv_w = (w_short[l], w_conf[l], b_conf[l], g_conf_ln[l], b_conf_ln[l])

        qx, kx, vtx, gatx, ybc = _inproj(x, 1.0 + sc, sh, g_l, w_in_b[l], rope_tabs, conv_w,
                                         kv_only=False, tm=tm_x)
        if not last:
            qc, kc, vtc, gatc, ybc_c = _inproj(ctx, 1.0 + sc_c, sh_c, g_l, w_in_b[l], None, conv_w,
                                               kv_only=False, tm=tm_c)
        else:
            kc, vtc = _inproj(ctx, 1.0 + sc_c, sh_c, g_l, w_in_b[l][:, K_OFF:GA_OFF], None, None,
                              kv_only=True, tm=tm_c)

        ot = _attention(qx, lam_params, (kc, kx), (vtc, vtx), lam_init=lam_init, qb=qb_x)
        x_new = _outproj(ot, gatx, g_sub, ybc, w_out_b[l], x, gt, g_final if last else None,
                         lam_init=lam_init, tm=tm_out)

        if not last:
            ot_c = _attention(qc, lam_params, (kc,), (vtc,), lam_init=lam_init, qb=qb_c)
            ctx = _outproj(ot_c, gatc, g_sub, ybc_c, w_out_b[l], ctx, gt_c, None, lam_init=lam_init, tm=tm_c)
        x = x_new
    return x
```
